```python
import jax, jax.numpy as jnp
from jax import lax
import numpy as np

D_MODEL = 1024
BATCH = 2
SEQ = 16384
DEPTH = 1
DEC_BATCH = 128
DEC_SEQ = 1
PAST_LEN = 8192
PAGE_SIZE = 128

NSA_HEADS = 8
NSA_KV_HEADS = 2
NSA_HPG = NSA_HEADS // NSA_KV_HEADS
NSA_HD = 64
NSA_WIDTH = NSA_HEADS * NSA_HD
NSA_KVW = NSA_KV_HEADS * NSA_HD
BLOCK = 64
N_SELECT = 16
N_LOCAL = 2
WINDOW = 512
Q_BLOCK = 128
FORCE_SCORE = float(NSA_HPG + 1)
GLA_HEADS = 4
GLA_DK = 64
GLA_DV = 128
GLA_WIDTH = GLA_HEADS * GLA_DV
GLA_GATE_RANK = 16
GLA_GATE_TEMP = 16.0
GLA_CHUNK = 64

MIX_WIDTH = NSA_WIDTH + GLA_WIDTH
D_FF = -(-8 * D_MODEL // (3 * 256)) * 256
PLE_DIM = 256
EPS = 1e-6

IN_SPLITS = (NSA_WIDTH, 4 * NSA_KVW, 2 * NSA_KVW, 3 * NSA_HEADS,
             GLA_HEADS * GLA_DK, GLA_HEADS * GLA_DK, GLA_WIDTH, GLA_GATE_RANK, GLA_WIDTH)
IN_WIDTH = sum(IN_SPLITS)

kernel_name = 'nsa_gla_parallel_heads_step'


def rmsnorm(x, g):
    xf = x.astype(jnp.float32)
    y = xf * lax.rsqrt(jnp.mean(xf * xf, axis=-1, keepdims=True) + EPS)
    return (y * g.astype(jnp.float32)).astype(x.dtype)


def masked_probs(s, mask, axis):
    s = jnp.where(mask, s.astype(jnp.float32), -jnp.inf)
    m = jnp.max(s, axis=axis, keepdims=True)
    m = jnp.where(jnp.isfinite(m), m, 0.0)
    p = jnp.exp(s - m)
    return p / jnp.maximum(jnp.sum(p, axis=axis, keepdims=True), 1e-30)


def split_in(z):
    offs = np.cumsum(IN_SPLITS)[:-1].tolist()
    return jnp.split(z, offs, axis=-1)


def compress_blocks(x, w, pe):
    n, t, g, d = x.shape
    xb = x.reshape(n, t // BLOCK, BLOCK, g, d) + pe[:, None, :]
    return jnp.einsum('nblgd,lde->nbge', xb, w)


def to_blocks(x):
    n, t, g, d = x.shape
    return x.reshape(n, t // BLOCK, BLOCK, g, d).transpose(0, 3, 1, 2, 4)


def gather_blocks(xb, idx):
    return jax.vmap(jax.vmap(lambda a, i: a[i]))(xb, idx)


def nsa_attend(q, gates, pos, ck, cv, skb, svb, wk, wv, wpos):
    n, nq = q.shape[:2]
    g, hpg, d = NSA_KV_HEADS, NSA_HPG, NSA_HD
    qg = q.reshape(n, nq, g, hpg, d) * (d ** -0.5)
    nb = ck.shape[1]
    nbs = skb.shape[2]
    n_sel = min(N_SELECT, nbs)
    s_c = jnp.einsum('nqghd,nbgd->nqghb', qg, ck)
    cblk = jnp.arange(nb)
    valid_c = (cblk[None, :] + 1) * BLOCK - 1 <= pos[:, None]
    p_c = masked_probs(s_c, valid_c[None, :, None, None, :], -1)
    o_c = jnp.einsum('nqghb,nbgd->nqghd', p_c.astype(cv.dtype), cv)
    imp = jnp.pad(jnp.sum(p_c, axis=3), ((0, 0), (0, 0), (0, 0), (0, nbs - nb)))
    sblk = jnp.arange(nbs)[None, :]
    cur = (pos // BLOCK)[:, None]
    causal_b = sblk <= cur
    forced = causal_b & ((sblk == 0) | (sblk > cur - N_LOCAL))
    score = jnp.where(forced[None, :, None, :], FORCE_SCORE,
                      jnp.where(causal_b[None, :, None, :], imp, -1.0))
    _, sel = lax.top_k(score, n_sel)
    idx = sel.transpose(0, 2, 1, 3).reshape(n, g, nq * n_sel)
    ks = gather_blocks(skb, idx).reshape(n, g, nq, n_sel, BLOCK, d)
    vs = gather_blocks(svb, idx).reshape(n, g, nq, n_sel, BLOCK, d)
    kpos = sel[..., None] * BLOCK + jnp.arange(BLOCK)
    mask_s = kpos <= pos[None, :, None, None, None]
    s_s = jnp.einsum('nqghd,ngqkld->nqghkl', qg, ks)
    p_s = masked_probs(s_s, mask_s[:, :, :, None], (-2, -1))
    o_s = jnp.einsum('nqghkl,ngqkld->nqghd', p_s.astype(vs.dtype), vs)
    s_w = jnp.einsum('nqghd,nwgd->nqghw', qg, wk)
    dist = pos[:, None] - wpos[None, :]
    mask_w = (dist >= 0) & (dist <= WINDOW) & (wpos[None, :] >= 0)
    p_w = masked_probs(s_w, mask_w[None, :, None, None, :], -1)
    o_w = jnp.einsum('nqghw,nwgd->nqghd', p_w.astype(wv.dtype), wv)
    gt = jax.nn.sigmoid(gates.astype(jnp.float32)).reshape(n, nq, g, hpg, 3)
    o = gt[..., 0:1] * o_c + gt[..., 1:2] * o_s + gt[..., 2:3] * o_w
    return o.reshape(n, nq, NSA_HEADS, d).astype(q.dtype)


def nsa_prompt(q, gates, kv, wkv, w_cmp, pe_cmp):
    n, t = q.shape[:2]
    ck = compress_blocks(kv[:, :, 0], w_cmp[0], pe_cmp[0])
    cv = compress_blocks(kv[:, :, 1], w_cmp[1], pe_cmp[1])
    skb = to_blocks(kv[:, :, 2])
    svb = to_blocks(kv[:, :, 3])
    wpad = jnp.pad(wkv, ((0, 0), (WINDOW, 0), (0, 0), (0, 0), (0, 0)))

    def one_block(i):
        start = i * Q_BLOCK
        qb = lax.dynamic_slice_in_dim(q, start, Q_BLOCK, axis=1)
        gb = lax.dynamic_slice_in_dim(gates, start, Q_BLOCK, axis=1)
        wb = lax.dynamic_slice_in_dim(wpad, start, Q_BLOCK + WINDOW, axis=1)
        pos = start + jnp.arange(Q_BLOCK, dtype=jnp.int32)
        wpos = start - WINDOW + jnp.arange(Q_BLOCK + WINDOW, dtype=jnp.int32)
        return nsa_attend(qb, gb, pos, ck, cv, skb, svb, wb[:, :, 0], wb[:, :, 1], wpos)

    o = lax.map(one_block, jnp.arange(t // Q_BLOCK, dtype=jnp.int32))
    return jnp.moveaxis(o, 0, 1).reshape(n, t, NSA_HEADS, NSA_HD)


def nsa_sample(q, gates, kv_new, wkv_new, kv_past, win_past, w_cmp, pe_cmp):
    n, ds = q.shape[:2]
    past_len = kv_past.shape[1]
    kv_past = kv_past.astype(kv_new.dtype)
    nb_new = ds // BLOCK
    new_c = kv_new[:, :nb_new * BLOCK]
    ck = jnp.concatenate([compress_blocks(kv_past[:, :, 0], w_cmp[0], pe_cmp[0]),
                          compress_blocks(new_c[:, :, 0], w_cmp[0], pe_cmp[0])], axis=1)
    cv = jnp.concatenate([compress_blocks(kv_past[:, :, 1], w_cmp[1], pe_cmp[1]),
                          compress_blocks(new_c[:, :, 1], w_cmp[1], pe_cmp[1])], axis=1)
    pad_new = -(-ds // BLOCK) * BLOCK - ds
    slc_new = jnp.pad(kv_new[:, :, 2:4], ((0, 0), (0, pad_new), (0, 0), (0, 0), (0, 0)))
    skb = jnp.concatenate([to_blocks(kv_past[:, :, 2]), to_blocks(slc_new[:, :, 0])], axis=2)
    svb = jnp.concatenate([to_blocks(kv_past[:, :, 3]), to_blocks(slc_new[:, :, 1])], axis=2)
    wall = jnp.concatenate([win_past.astype(wkv_new.dtype), wkv_new], axis=1)
    wkeep = win_past.shape[1]
    wpos = past_len - wkeep + jnp.arange(wkeep + ds, dtype=jnp.int32)
    pos = past_len + jnp.arange(ds, dtype=jnp.int32)
    o = nsa_attend(q, gates, pos, ck, cv, skb, svb, wall[:, :, 0], wall[:, :, 1], wpos)
    return o, wall[:, ds:]


def gla_scan(q, k, v, log_a, s0):
    n, t, h, dk = q.shape
    dv = v.shape[-1]
    c = min(GLA_CHUNK, t)
    tp = -(-t // c) * c
    padw = ((0, 0), (0, tp - t), (0, 0), (0, 0))
    q, k, v, log_a = [jnp.pad(a, padw) for a in (q, k, v, log_a)]
    nc = tp // c

    def to_chunks(a):
        return jnp.moveaxis(a.reshape(n, nc, c, h, a.shape[-1]), 1, 0)

    causal = jnp.tril(jnp.ones((c, c), dtype=bool))

    def step(s, inp):
        qc, kc, vc, ac = inp
        b = jnp.cumsum(ac, axis=1)
        o_inter = jnp.einsum('nchk,nhkv->nchv', qc * jnp.exp(b), s)
        diff = jnp.where(causal[None, :, :, None, None], b[:, :, None] - b[:, None, :], -jnp.inf)
        att = jnp.einsum('nthk,nshk,ntshk->nths', qc, kc, jnp.exp(diff))
        o_intra = jnp.einsum('nths,nshv->nthv', att, vc)
        b_last = b[:, -1]
        s_new = jnp.exp(b_last)[..., None] * s + jnp.einsum(
            'nshk,nshv->nhkv', kc * jnp.exp(b_last[:, None] - b), vc)
        return s_new, o_inter + o_intra

    s_fin, o = lax.scan(step, s0, (to_chunks(q), to_chunks(k), to_chunks(v), to_chunks(log_a)))
    o = jnp.moveaxis(o, 0, 1).reshape(n, tp, h, dv)[:, :t]
    return o, s_fin


def gla_mix(q, k, v, a_lr, s0, w_a2, b_a):
    n, t = q.shape[:2]
    f32 = jnp.float32
    qf = q.reshape(n, t, GLA_HEADS, GLA_DK).astype(f32) * (GLA_DK ** -0.5)
    kf = k.reshape(n, t, GLA_HEADS, GLA_DK).astype(f32)
    vf = v.reshape(n, t, GLA_HEADS, GLA_DV).astype(f32)
    log_a = (jax.nn.log_sigmoid((a_lr @ w_a2 + b_a).astype(f32)) / GLA_GATE_TEMP).reshape(
        n, t, GLA_HEADS, GLA_DK)
    o, s = gla_scan(qf, kf, vf, log_a, s0.astype(f32))
    return o.astype(q.dtype), s


def project(x, g_attn, w_in):
    n, t = x.shape[:2]
    q, kv, wkv, gts, gq, gk, gv, ga, gr = split_in(rmsnorm(x, g_attn) @ w_in)
    return (q.reshape(n, t, NSA_HEADS, NSA_HD), gts.reshape(n, t, NSA_HEADS, 3),
            kv.reshape(n, t, 4, NSA_KV_HEADS, NSA_HD), wkv.reshape(n, t, 2, NSA_KV_HEADS, NSA_HD),
            gq, gk, gv, ga, gr)


def mixer_out(o_nsa, o_gla, r, g_nsa_out, g_gla_out, w_out):
    n, t = o_nsa.shape[:2]
    on = rmsnorm(o_nsa, g_nsa_out).reshape(n, t, NSA_WIDTH)
    og = rmsnorm(o_gla, g_gla_out).reshape(n, t, GLA_WIDTH) * jax.nn.silu(r)
    return jnp.concatenate([on, og], axis=-1) @ w_out


def ffn_ple(x, p, g_ffn, w_gate_up, w_down, g_ple, w_ple_gate, w_ple_proj):
    h = rmsnorm(x, g_ffn)
    gt, up = jnp.split(h @ w_gate_up, 2, axis=-1)
    x = x + (jax.nn.silu(gt) * up) @ w_down
    gate = jax.nn.sigmoid((rmsnorm(x, g_ple) @ w_ple_gate).astype(jnp.float32))
    return x + ((p @ w_ple_proj) * gate).astype(x.dtype)


def setup_inputs(seed: int = 0) -> dict:
    key = jax.random.key(seed)
    ks = jax.random.split(key, 24)
    f32 = jnp.float32

    def nrm(k, shape, scale):
        return scale * jax.random.normal(k, shape, f32)

    n_pages = PAST_LEN // PAGE_SIZE
    n_used = DEC_BATCH * n_pages
    n_pool = n_used + n_used // 4
    wkeep = min(WINDOW, PAST_LEN)
    page_table = jax.random.permutation(ks[5], n_pool)[:n_used].reshape(
        DEC_BATCH, n_pages).astype(jnp.int32)
    return {
        'x_prompt': nrm(ks[0], (BATCH, SEQ, D_MODEL), 1.0),
        'x_sample': nrm(ks[1], (DEC_BATCH, DEC_SEQ, D_MODEL), 1.0),
        'cache_kv': nrm(ks[2], (DEPTH, n_pool, PAGE_SIZE, 4, NSA_KV_HEADS, NSA_HD), 1.0),
        'cache_win': nrm(ks[3], (DEPTH, DEC_BATCH, wkeep, 2, NSA_KV_HEADS, NSA_HD), 1.0),
        'state_gla': nrm(ks[4], (DEPTH, DEC_BATCH, GLA_HEADS, GLA_DK, GLA_DV), 0.5),
        'page_table': page_table,
        'p_prompt': nrm(ks[6], (DEPTH, BATCH, SEQ, PLE_DIM), 1.0),
        'p_sample': nrm(ks[7], (DEPTH, DEC_BATCH, DEC_SEQ, PLE_DIM), 1.0),
        'g_attn': 1.0 + nrm(ks[8], (DEPTH, D_MODEL), 0.05),
        'w_in': nrm(ks[9], (DEPTH, D_MODEL, IN_WIDTH), D_MODEL ** -0.5),
        'w_cmp': nrm(ks[10], (DEPTH, 2, BLOCK, NSA_HD, NSA_HD), (BLOCK * NSA_HD) ** -0.5),
        'pe_cmp': nrm(ks[11], (DEPTH, 2, BLOCK, NSA_HD), 0.1),
        'g_nsa_out': 1.0 + nrm(ks[12], (DEPTH, NSA_HEADS, NSA_HD), 0.05),
        'w_gla_a2': nrm(ks[13], (DEPTH, GLA_GATE_RANK, GLA_HEADS * GLA_DK), GLA_GATE_RANK ** -0.5),
        'b_gla_a': nrm(ks[14], (DEPTH, GLA_HEADS * GLA_DK), 0.1),
        'g_gla_out': 1.0 + nrm(ks[15], (DEPTH, GLA_HEADS, GLA_DV), 0.05),
        'w_out': nrm(ks[16], (DEPTH, MIX_WIDTH, D_MODEL), MIX_WIDTH ** -0.5),
        'g_ffn': 1.0 + nrm(ks[17], (DEPTH, D_MODEL), 0.05),
        'w_gate_up': nrm(ks[18], (DEPTH, D_MODEL, 2 * D_FF), D_MODEL ** -0.5),
        'w_down': nrm(ks[19], (DEPTH, D_FF, D_MODEL), D_FF ** -0.5),
        'g_ple': 1.0 + nrm(ks[20], (DEPTH, D_MODEL), 0.05),
        'w_ple_gate': nrm(ks[21], (DEPTH, D_MODEL, D_MODEL), D_MODEL ** -0.5),
        'w_ple_proj': nrm(ks[22], (DEPTH, PLE_DIM, D_MODEL), PLE_DIM ** -0.5),
        'g_final': 1.0 + nrm(ks[23], (D_MODEL,), 0.05),
    }


def reference(x_prompt, x_sample, cache_kv, cache_win, state_gla, page_table, p_prompt, p_sample,
              g_attn, w_in, w_cmp, pe_cmp, g_nsa_out, w_gla_a2, b_gla_a, g_gla_out, w_out,
              g_ffn, w_gate_up, w_down, g_ple, w_ple_gate, w_ple_proj, g_final):
    xp, xs = x_prompt, x_sample
    kv_p, win_p, gla_p, kv_s, win_s, gla_s = [], [], [], [], [], []
    for l in range(DEPTH):
        q, gts, kv, wkv, gq, gk, gv, ga, gr = project(xp, g_attn[l], w_in[l])
        o_n = nsa_prompt(q, gts, kv, wkv, w_cmp[l], pe_cmp[l])
        s0 = jnp.zeros((xp.shape[0], GLA_HEADS, GLA_DK, GLA_DV), jnp.float32)
        o_g, s_p = gla_mix(gq, gk, gv, ga, s0, w_gla_a2[l], b_gla_a[l])
        xp = xp + mixer_out(o_n, o_g, gr, g_nsa_out[l], g_gla_out[l], w_out[l])
        xp = ffn_ple(xp, p_prompt[l], g_ffn[l], w_gate_up[l], w_down[l], g_ple[l],
                     w_ple_gate[l], w_ple_proj[l])
        kv_p.append(kv)
        win_p.append(wkv[:, -min(WINDOW, wkv.shape[1]):])
        gla_p.append(s_p)
        q, gts, kv, wkv, gq, gk, gv, ga, gr = project(xs, g_attn[l], w_in[l])
        kv_past = cache_kv[l, page_table]
        kv_past = kv_past.reshape(xs.shape[0], -1, 4, NSA_KV_HEADS, NSA_HD)
        o_n, win_new = nsa_sample(q, gts, kv, wkv, kv_past, cache_win[l], w_cmp[l], pe_cmp[l])
        o_g, s_s = gla_mix(gq, gk, gv, ga, state_gla[l], w_gla_a2[l], b_gla_a[l])
        xs = xs + mixer_out(o_n, o_g, gr, g_nsa_out[l], g_gla_out[l], w_out[l])
        xs = ffn_ple(xs, p_sample[l], g_ffn[l], w_gate_up[l], w_down[l], g_ple[l],
                     w_ple_gate[l], w_ple_proj[l])
        kv_s.append(kv)
        win_s.append(win_new)
        gla_s.append(s_s)
    y_prompt = rmsnorm(xp, g_final)
    y_sample = rmsnorm(xs, g_final)
    return (y_prompt, y_sample, jnp.stack(kv_p), jnp.stack(win_p), jnp.stack(gla_p),
            jnp.stack(kv_s), jnp.stack(win_s), jnp.stack(gla_s))
```

```python
import functools

import numpy as np
import jax
import jax.numpy as jnp
from jax import lax
from jax.experimental import pallas as pl
from jax.experimental.pallas import tpu as pltpu

F32 = jnp.float32
BF16 = jnp.bfloat16

EPS = 1e-6
NEG = -1e30

NSA_HEADS = 8
NSA_KV_HEADS = 2
NSA_HPG = NSA_HEADS // NSA_KV_HEADS
NSA_HD = 64
NSA_WIDTH = NSA_HEADS * NSA_HD
NSA_KVW = NSA_KV_HEADS * NSA_HD
BLOCK = 64
N_SELECT = 16
N_LOCAL = 2
WINDOW = 512
Q_BLOCK = 128
FORCE_SCORE = float(NSA_HPG + 1)
GLA_HEADS = 4
GLA_DK = 64
GLA_DV = 128
GLA_WIDTH = GLA_HEADS * GLA_DV
GLA_KW = GLA_HEADS * GLA_DK
GLA_GATE_RANK = 16
GLA_GATE_TEMP = 16.0
GLA_CHUNK = 64
N_GATES = 3 * NSA_HEADS
MISC_W = 128

VMEM_LIMIT = 56 * 1024 * 1024


def _cparams(sem):
    return pltpu.CompilerParams(dimension_semantics=sem, vmem_limit_bytes=VMEM_LIMIT)


def _const_spec(shape):
    nd = len(shape)
    return pl.BlockSpec(shape, lambda *_: (0,) * nd, pipeline_mode=pl.Buffered(1))


def _sigmoid(x):
    return 1.0 / (1.0 + jnp.exp(-x))


def _rms(x, g):
    ms = jnp.mean(x * x, axis=-1, keepdims=True)
    return x * lax.rsqrt(ms + EPS) * g


def _split3(x):
    hi = x.astype(BF16)
    r = x - hi.astype(F32)
    mid = r.astype(BF16)
    lo = (r - mid.astype(F32)).astype(BF16)
    return hi, mid, lo


_P_Q, _P_KV, _P_WKV, _P_GQ, _P_GK, _P_GV, _P_GR, _P_MISC, _P_END = (
    0, 512, 1024, 1280, 1536, 1792, 2304, 2816, 2944)


def _reorder_w_in(w_in):
    offs = np.cumsum([0, NSA_WIDTH, 4 * NSA_KVW, 2 * NSA_KVW, N_GATES, GLA_KW, GLA_KW,
                      GLA_WIDTH, GLA_GATE_RANK, GLA_WIDTH])
    q, kv, wkv, gts, gq, gk, gv, ga, gr = [w_in[:, offs[i]:offs[i + 1]] for i in range(9)]
    pad = jnp.zeros((w_in.shape[0], MISC_W - N_GATES - GLA_GATE_RANK), w_in.dtype)
    return jnp.concatenate([q, kv, wkv, gq, gk, gv, gr, gts, ga, pad], axis=1).astype(BF16)


def _proj_kernel(x_ref, g_ref, w_ref, q_ref, kv_ref, wkv_ref, gq_ref, gk_ref, gv_ref, gr_ref,
                 misc_ref):
    xn = _rms(x_ref[...], g_ref[...]).astype(BF16)
    for ref, a, b in ((q_ref, _P_Q, _P_KV), (kv_ref, _P_KV, _P_WKV), (wkv_ref, _P_WKV, _P_GQ),
                      (gq_ref, _P_GQ, _P_GK), (gk_ref, _P_GK, _P_GV), (gv_ref, _P_GV, _P_GR),
                      (gr_ref, _P_GR, _P_MISC), (misc_ref, _P_MISC, _P_END)):
        ref[...] = jnp.dot(xn, w_ref[:, a:b], preferred_element_type=F32)


def _project(x2d, g_attn, w1, tm):
    t, d = x2d.shape
    widths = (512, 512, 256, 256, 256, 512, 512, MISC_W)
    return pl.pallas_call(
        _proj_kernel,
        grid=(t // tm,),
        in_specs=[pl.BlockSpec((tm, d), lambda i: (i, 0)),
                  _const_spec((1, d)),
                  _const_spec(w1.shape)],
        out_specs=[pl.BlockSpec((tm, w), lambda i: (i, 0)) for w in widths],
        out_shape=[jax.ShapeDtypeStruct((t, w), F32) for w in widths],
        compiler_params=_cparams(("parallel",)),
        name="proj",
    )(x2d, g_attn.reshape(1, d), w1)


def _group_mean_sq(x, bd_ref, width):
    hi, mid, lo = _split3(x * x)
    bd = bd_ref[...]
    s = (jnp.dot(hi, bd, preferred_element_type=F32) + jnp.dot(mid, bd, preferred_element_type=F32)
         + jnp.dot(lo, bd, preferred_element_type=F32))
    return s * (1.0 / width)


def _ffn_kernel(on_ref, og_ref, gr_ref, x_ref, p_ref, gn_ref, gg_ref, bd64_ref, bd128_ref,
                wo_ref, gffn_ref, wgu_ref, wd_ref, gple_ref, wpg_ref, wpp_ref, gfin_ref,
                y_ref, h_ref, acc_ref, *, n_ff):
    o_n = on_ref[...]
    o_g = og_ref[...]
    r = gr_ref[...]
    a_n = o_n * lax.rsqrt(_group_mean_sq(o_n, bd64_ref, NSA_HD) + EPS) * gn_ref[...]
    a_g = o_g * lax.rsqrt(_group_mean_sq(o_g, bd128_ref, GLA_DV) + EPS) * gg_ref[...]
    a_g = a_g * (r * _sigmoid(r))
    x1 = (x_ref[...]
          + jnp.dot(a_n.astype(BF16), wo_ref[0:NSA_WIDTH, :], preferred_element_type=F32)
          + jnp.dot(a_g.astype(BF16), wo_ref[NSA_WIDTH:, :], preferred_element_type=F32))
    h_ref[...] = _rms(x1, gffn_ref[...]).astype(BF16)
    acc_ref[...] = x1

    def body(c, carry):
        gu = jnp.dot(h_ref[...], wgu_ref[c], preferred_element_type=F32)
        half = gu.shape[1] // 2
        gt = gu[:, :half]
        act = (gt * _sigmoid(gt) * gu[:, half:]).astype(BF16)
        acc_ref[...] += jnp.dot(act, wd_ref[c], preferred_element_type=F32)
        return carry

    lax.fori_loop(0, n_ff, body, 0)
    x2 = acc_ref[...]
    gate = _sigmoid(jnp.dot(_rms(x2, gple_ref[...]).astype(BF16), wpg_ref[...],
                            preferred_element_type=F32))
    x3 = x2 + jnp.dot(p_ref[...].astype(BF16), wpp_ref[...], preferred_element_type=F32) * gate
    y_ref[...] = _rms(x3, gfin_ref[...])


def _block_diag_ones(n, width):
    idx = np.arange(n) // width
    return jnp.asarray((idx[:, None] == idx[None, :]).astype(np.float32), BF16)


def _ffn_weights(g_nsa_out, g_gla_out, w_out, g_ffn, w_gate_up, w_down, g_ple, w_ple_gate,
                 w_ple_proj, g_final, ff_chunk=256):
    d = w_out.shape[1]
    d_ff = w_down.shape[0]
    n_ff = d_ff // ff_chunk
    wg = w_gate_up[:, :d_ff].reshape(d, n_ff, ff_chunk)
    wu = w_gate_up[:, d_ff:].reshape(d, n_ff, ff_chunk)
    wgu = jnp.concatenate([wg, wu], axis=2).transpose(1, 0, 2).astype(BF16)
    wd = w_down.reshape(n_ff, ff_chunk, d).astype(BF16)
    return dict(gn=g_nsa_out.reshape(1, NSA_WIDTH), gg=g_gla_out.reshape(1, GLA_WIDTH),
                bd64=_block_diag_ones(NSA_WIDTH, NSA_HD), bd128=_block_diag_ones(GLA_WIDTH, GLA_DV),
                wo=w_out.astype(BF16), gffn=g_ffn.reshape(1, d), wgu=wgu, wd=wd,
                gple=g_ple.reshape(1, d), wpg=w_ple_gate.astype(BF16),
                wpp=w_ple_proj.astype(BF16), gfin=g_final.reshape(1, d))


def _mixer_ffn(o_n, o_g, gr, x2d, p2d, fw, tm):
    t, d = x2d.shape
    n_ff = fw["wgu"].shape[0]
    consts = [fw[k] for k in ("gn", "gg", "bd64", "bd128", "wo", "gffn", "wgu", "wd", "gple",
                              "wpg", "wpp", "gfin")]
    row = lambda w: pl.BlockSpec((tm, w), lambda i: (i, 0))
    return pl.pallas_call(
        functools.partial(_ffn_kernel, n_ff=n_ff),
        grid=(t // tm,),
        in_specs=[row(NSA_WIDTH), row(GLA_WIDTH), row(GLA_WIDTH), row(d), row(p2d.shape[1])]
                 + [_const_spec(c.shape) for c in consts],
        out_specs=row(d),
        out_shape=jax.ShapeDtypeStruct((t, d), F32),
        scratch_shapes=[pltpu.VMEM((tm, d), BF16), pltpu.VMEM((tm, d), F32)],
        compiler_params=_cparams(("parallel",)),
        name="mixer_ffn",
    )(o_n, o_g, gr, x2d, p2d, *consts)


def _cmp_weights(w_cmp, pe_cmp):
    z = jnp.zeros((BLOCK, NSA_HD, NSA_HD), w_cmp.dtype)

    def bd(w):
        return jnp.concatenate([jnp.concatenate([w, z], axis=2),
                                jnp.concatenate([z, w], axis=2)], axis=1)

    w2 = jnp.stack([bd(w_cmp[0]), bd(w_cmp[1])], axis=1).astype(BF16)
    pe = jnp.stack([jnp.concatenate([pe_cmp[0], pe_cmp[0]], axis=1),
                    jnp.concatenate([pe_cmp[1], pe_cmp[1]], axis=1)], axis=1)
    return w2, pe.reshape(BLOCK, 2, 1, NSA_KVW)


def _compress(xk_ref, xv_ref, w2_ref, pe_ref, nb):
    acc_k = jnp.zeros((nb, NSA_KVW), F32)
    acc_v = jnp.zeros((nb, NSA_KVW), F32)
    for l in range(BLOCK):
        xk = (xk_ref[pl.ds(l, nb, stride=BLOCK), :] + pe_ref[l, 0]).astype(BF16)
        xv = (xv_ref[pl.ds(l, nb, stride=BLOCK), :] + pe_ref[l, 1]).astype(BF16)
        acc_k += jnp.dot(xk, w2_ref[l, 0], preferred_element_type=F32)
        acc_v += jnp.dot(xv, w2_ref[l, 1], preferred_element_type=F32)
    return acc_k, acc_v


def _prep_kernel(ckin_ref, cvin_ref, ksin_ref, vsin_ref, wkv_ref, w2_ref, pe_ref,
                 c_ref, ks_ref, vst_ref, kw_ref, vwt_ref, *, tk):
    nb = tk // BLOCK
    ck, cv = _compress(ckin_ref, cvin_ref, w2_ref, pe_ref, nb)
    c_ref[:, 0:NSA_KVW] = ck
    c_ref[:, NSA_KVW:] = cv
    ch = 512
    for c in range(tk // ch):
        rows = slice(c * ch, (c + 1) * ch)
        ks_ref[rows, :] = ksin_ref[rows, :].astype(BF16)
        vst_ref[0, :, rows] = vsin_ref[rows, :].T.astype(BF16)
        kw_ref[rows, :] = wkv_ref[rows, 0:NSA_KVW].astype(BF16)
        vwt_ref[0, :, rows] = wkv_ref[rows, NSA_KVW:].T.astype(BF16)


def _nsa_prep(kv2d, wkv2d, w2, pe, n, s, tk):
    t = n * s
    nj = s // tk
    kv_specs = [pl.BlockSpec((tk, NSA_KVW), (lambda b, j, c=c: (b * nj + j, c))) for c in range(4)]
    return pl.pallas_call(
        functools.partial(_prep_kernel, tk=tk),
        grid=(n, nj),
        in_specs=kv_specs + [pl.BlockSpec((tk, 2 * NSA_KVW), lambda b, j: (b * nj + j, 0)),
                             _const_spec(w2.shape), _const_spec(pe.shape)],
        out_specs=[pl.BlockSpec((tk // BLOCK, 2 * NSA_KVW), lambda b, j: (b * nj + j, 0)),
                   pl.BlockSpec((tk, NSA_KVW), lambda b, j: (b * nj + j, 0)),
                   pl.BlockSpec((1, NSA_KVW, tk), lambda b, j: (b, 0, j)),
                   pl.BlockSpec((tk, NSA_KVW), lambda b, j: (b * nj + j, 0)),
                   pl.BlockSpec((1, NSA_KVW, tk), lambda b, j: (b, 0, j))],
        out_shape=[jax.ShapeDtypeStruct((t // BLOCK, 2 * NSA_KVW), F32),
                   jax.ShapeDtypeStruct((t, NSA_KVW), BF16),
                   jax.ShapeDtypeStruct((n, NSA_KVW, s), BF16),
                   jax.ShapeDtypeStruct((t, NSA_KVW), BF16),
                   jax.ShapeDtypeStruct((n, NSA_KVW, s), BF16)],
        compiler_params=_cparams(("parallel", "parallel")),
        name="nsa_prep",
    )(kv2d, kv2d, kv2d, kv2d, wkv2d, w2, pe)


_LQ = NSA_KV_HEADS * NSA_HPG * Q_BLOCK
_LG = NSA_HPG * Q_BLOCK


def _rep_heads(x):
    a, b = x[:, :Q_BLOCK], x[:, Q_BLOCK:]
    return jnp.concatenate([a] * NSA_HPG + [b] * NSA_HPG, axis=1)


def _flash_tile(k, vt, qt, carry, bias=None, mask=None):
    m, l, acc0, acc1 = carry
    s = jnp.dot(k, qt, preferred_element_type=F32)
    if bias is not None:
        s = s + bias
    if mask is not None:
        s = jnp.where(mask, s, NEG)
    m_new = jnp.maximum(m, jnp.max(s, axis=0, keepdims=True))
    alpha = jnp.exp(m - m_new)
    p = jnp.exp(s - m_new)
    l = alpha * l + jnp.sum(p, axis=0, keepdims=True)
    pb = p.astype(BF16)
    acc0 = alpha[:, :_LG] * acc0 + jnp.dot(vt[:NSA_HD], pb[:, :_LG], preferred_element_type=F32)
    acc1 = alpha[:, _LG:] * acc1 + jnp.dot(vt[NSA_HD:], pb[:, _LG:], preferred_element_type=F32)
    return m_new, l, acc0, acc1


def _flash_init():
    return (jnp.full((1, _LQ), NEG, F32), jnp.zeros((1, _LQ), F32),
            jnp.zeros((NSA_HD, _LG), F32), jnp.zeros((NSA_HD, _LG), F32))


def _flash_out(carry):
    m, l, acc0, acc1 = carry
    inv = 1.0 / jnp.maximum(l, 1e-30)
    return acc0 * inv[:, :_LG], acc1 * inv[:, _LG:]


def _select_blocks(imp, cur, nbs):
    bi = lax.broadcasted_iota(jnp.int32, imp.shape, 0)
    bf = bi.astype(F32)
    causal = bi <= cur
    forced = jnp.logical_and(causal, jnp.logical_or(bi == 0, bi > cur - N_LOCAL))
    score = jnp.where(forced, FORCE_SCORE, jnp.where(causal, imp, -1.0))

    def body(_, c):
        score, sel = c
        mx = jnp.max(score, axis=0, keepdims=True)
        idx = jnp.min(jnp.where(score == mx, bf, float(nbs)), axis=0, keepdims=True)
        pick = bf == idx
        return jnp.where(pick, -2.0, score), jnp.where(pick, 1.0, sel)

    _, sel = lax.fori_loop(0, min(N_SELECT, nbs), body, (score, jnp.zeros(imp.shape, F32)))
    return sel


def _nsa_prompt_kernel(q_ref, misc_ref, c_ref, ks_ref, vst_ref, *rest, nbc):
    kw_refs = rest[0:5]
    vwt_refs = rest[5:10]
    o_ref = rest[10]
    selb_ref = rest[11]
    i = pl.program_id(1)

    qT = (q_ref[...] * (NSA_HD ** -0.5)).T
    z = jnp.zeros((NSA_HD, Q_BLOCK), F32)
    top = [qT[NSA_HD * h:NSA_HD * (h + 1)] for h in range(NSA_HPG)] + [z] * NSA_HPG
    bot = [z] * NSA_HPG + [qT[NSA_HD * h:NSA_HD * (h + 1)] for h in range(NSA_HPG, NSA_HEADS)]
    qt = jnp.concatenate([jnp.concatenate(top, axis=1), jnp.concatenate(bot, axis=1)],
                         axis=0).astype(BF16)

    cmat = c_ref[...]
    ck = cmat[:, :NSA_KVW].astype(BF16)
    cv = cmat[:, NSA_KVW:].astype(BF16)
    sc = jnp.dot(ck, qt, preferred_element_type=F32)
    b_io = lax.broadcasted_iota(jnp.int32, (nbc, _LQ), 0)
    qoff = jnp.bitwise_and(lax.broadcasted_iota(jnp.int32, (nbc, _LQ), 1), Q_BLOCK - 1)
    valid = b_io * BLOCK + (BLOCK - 1) <= i * Q_BLOCK + qoff
    sc = jnp.where(valid, sc, NEG)
    mc = jnp.max(sc, axis=0, keepdims=True)
    pc = jnp.where(valid, jnp.exp(sc - mc), 0.0)
    pc = pc / jnp.maximum(jnp.sum(pc, axis=0, keepdims=True), 1e-30)
    oc_full = lax.dot_general(cv, pc.astype(BF16), (((0,), (0,)), ((), ())),
                              preferred_element_type=F32)
    oc = (oc_full[:NSA_HD, :_LG], oc_full[NSA_HD:, _LG:])

    qo = lax.broadcasted_iota(jnp.int32, (1, Q_BLOCK), 1)
    cur = 2 * i + (qo >= BLOCK).astype(jnp.int32)
    for g in range(NSA_KV_HEADS):
        imp = pc[:, g * _LG:g * _LG + Q_BLOCK]
        for hh in range(1, NSA_HPG):
            imp = imp + pc[:, g * _LG + hh * Q_BLOCK:g * _LG + (hh + 1) * Q_BLOCK]
        sel = _select_blocks(imp, cur, nbc)
        selb_ref[:, g * Q_BLOCK:(g + 1) * Q_BLOCK] = jnp.where(sel > 0.0, 0.0, NEG)

    row = lax.broadcasted_iota(jnp.int32, (Q_BLOCK, _LQ), 0)
    qcol = jnp.bitwise_and(lax.broadcasted_iota(jnp.int32, (Q_BLOCK, _LQ), 1), Q_BLOCK - 1)

    def sel_tile(jt, carry, mask):
        off = pl.multiple_of(jt * Q_BLOCK, Q_BLOCK)
        k = ks_ref[pl.ds(off, Q_BLOCK), :]
        vt = vst_ref[0, :, pl.ds(off, Q_BLOCK)]
        b0 = jnp.broadcast_to(selb_ref[pl.ds(2 * jt, 1), :], (BLOCK, 2 * Q_BLOCK))
        b1 = jnp.broadcast_to(selb_ref[pl.ds(2 * jt + 1, 1), :], (BLOCK, 2 * Q_BLOCK))
        bias = _rep_heads(jnp.concatenate([b0, b1], axis=0))
        return _flash_tile(k, vt, qt, carry, bias=bias, mask=mask)

    carry = lax.fori_loop(0, i, lambda jt, c: sel_tile(jt, c, None), _flash_init())
    carry = sel_tile(i, carry, row <= qcol)
    os_ = _flash_out(carry)

    carry = _flash_tile(kw_refs[4][...], vwt_refs[4][0], qt, _flash_init(), mask=row <= qcol)
    for t in range(4):
        first_ok = jnp.where(i - 4 + t >= 0, 0, Q_BLOCK)
        mask = row >= (qcol + first_ok if t == 0 else first_ok)
        carry = _flash_tile(kw_refs[t][...], vwt_refs[t][0], qt, carry, mask=mask)
    ow = _flash_out(carry)

    gt = _sigmoid(misc_ref[...]).T
    outs = []
    for g in range(NSA_KV_HEADS):
        for hh in range(NSA_HPG):
            h = g * NSA_HPG + hh
            sl = slice(hh * Q_BLOCK, (hh + 1) * Q_BLOCK)
            outs.append(gt[3 * h:3 * h + 1] * oc[g][:, sl] + gt[3 * h + 1:3 * h + 2] * os_[g][:, sl]
                        + gt[3 * h + 2:3 * h + 3] * ow[g][:, sl])
    o_ref[...] = jnp.concatenate(outs, axis=0).T


def _nsa_prompt(q2d, misc2d, cmat, ks, vst, kw, vwt, n, s):
    nq = s // Q_BLOCK
    nbc = s // BLOCK
    wk_specs = [pl.BlockSpec((Q_BLOCK, NSA_KVW),
                             (lambda b, i, t=t: (b * nq + jnp.maximum(i - 4 + t, 0), 0)))
                for t in range(5)]
    wv_specs = [pl.BlockSpec((1, NSA_KVW, Q_BLOCK),
                             (lambda b, i, t=t: (b, 0, jnp.maximum(i - 4 + t, 0))))
                for t in range(5)]
    return pl.pallas_call(
        functools.partial(_nsa_prompt_kernel, nbc=nbc),
        grid=(n, nq),
        in_specs=[pl.BlockSpec((Q_BLOCK, NSA_WIDTH), lambda b, i: (b * nq + i, 0)),
                  pl.BlockSpec((Q_BLOCK, MISC_W), lambda b, i: (b * nq + i, 0)),
                  pl.BlockSpec((nbc, 2 * NSA_KVW), lambda b, i: (b, 0)),
                  pl.BlockSpec((s, NSA_KVW), lambda b, i: (b, 0)),
                  pl.BlockSpec((1, NSA_KVW, s), lambda b, i: (b, 0, 0))] + wk_specs + wv_specs,
        out_specs=pl.BlockSpec((Q_BLOCK, NSA_WIDTH), lambda b, i: (b * nq + i, 0)),
        out_shape=jax.ShapeDtypeStruct((n * s, NSA_WIDTH), F32),
        scratch_shapes=[pltpu.VMEM((nbc, 2 * Q_BLOCK), F32)],
        compiler_params=_cparams(("parallel", "arbitrary")),
        name="nsa_prompt",
    )(q2d, misc2d, cmat, ks, vst, *([kw] * 5), *([vwt] * 5))


_N_LEVELS = int(np.log2(GLA_CHUNK))


def _gla_consts():
    c = GLA_CHUNK
    t = np.arange(c)
    mats = [t[None, :] <= t[:, None], t[None, :] > t[:, None]]
    masks = []
    w = c // 2
    while w >= 1:
        blk = t // (2 * w)
        mid = blk * 2 * w + w
        upper = t >= mid
        m = np.zeros((c, c), bool)
        for r in range(c):
            if upper[r]:
                m[r, mid[r]:r + 1] = True
            else:
                m[r, r + 1:mid[r]] = True
        mats.append(m)
        masks.append((blk[:, None] == blk[None, :]) & upper[:, None] & ~upper[None, :])
        w //= 2
    masks.append(np.eye(c, dtype=bool))
    mall = jnp.asarray(np.concatenate(mats, axis=0).astype(np.float32), BF16)
    lmask = jnp.asarray(np.tile(np.stack(masks).astype(np.float32), (1, GLA_HEADS, 1)))
    hd = np.arange(GLA_KW) // GLA_DK
    hmask = jnp.asarray((hd[:, None] == hd[None, :]).astype(np.float32))
    return mall, lmask, hmask


def _gla_gate_weights(w_a2, b_a):
    w2 = jnp.zeros((MISC_W, GLA_KW), F32).at[N_GATES:N_GATES + GLA_GATE_RANK].set(w_a2)
    return w2.astype(BF16), b_a.reshape(1, GLA_KW)


def _log_decay(misc, w2_ref, ba_ref):
    x = jnp.dot(misc.astype(BF16), w2_ref[...], preferred_element_type=F32) + ba_ref[...]
    return (jnp.minimum(x, 0.0) - jnp.log1p(jnp.exp(-jnp.abs(x)))) * (1.0 / GLA_GATE_TEMP)


def _gla_kernel(gq_ref, gk_ref, gv_ref, misc_ref, w2_ref, ba_ref, mall_ref, lmask_ref, hm_ref,
                o_ref, sout_ref, s_ref, *, nchunk):
    j = pl.program_id(1)

    @pl.when(j == 0)
    def _():
        s_ref[...] = jnp.zeros(s_ref.shape, F32)

    la_all = _log_decay(misc_ref[...], w2_ref, ba_ref)
    mall = mall_ref[...]
    hm = hm_ref[...]
    c = GLA_CHUNK
    dims_t = (((1,), (1,)), ((), ()))

    def stack(x):
        return (jnp.concatenate([x] * GLA_HEADS, axis=0) * hm).astype(BF16)

    for ci in range(nchunk):
        rows = slice(ci * c, (ci + 1) * c)
        q = gq_ref[rows, :] * (GLA_DK ** -0.5)
        k = gk_ref[rows, :]
        v = gv_ref[rows, :].astype(BF16)
        la = la_all[rows]
        hi, mid, lo = _split3(la)
        ex = jnp.exp(jnp.dot(mall, hi, preferred_element_type=F32)
                     + jnp.dot(mall, mid, preferred_element_type=F32)
                     + jnp.dot(mall, lo, preferred_element_type=F32))
        att = lax.dot_general(stack(q), k.astype(BF16), dims_t,
                              preferred_element_type=F32) * lmask_ref[_N_LEVELS]
        for lev in range(_N_LEVELS):
            e = ex[(2 + lev) * c:(3 + lev) * c]
            att += lax.dot_general(stack(q * e), (k * e).astype(BF16), dims_t,
                                   preferred_element_type=F32) * lmask_ref[lev]
        attb = att.astype(BF16)
        s_old = s_ref[...]
        o_inter = jnp.dot(stack(q * ex[0:c]), s_old.astype(BF16), preferred_element_type=F32)
        kd = (k * ex[c:2 * c]).astype(BF16)
        upd = lax.dot_general(kd, v, (((0,), (0,)), ((), ())), preferred_element_type=F32)
        dec = jnp.exp(jnp.sum(la.T, axis=1, keepdims=True))
        outs, news = [], []
        for h in range(GLA_HEADS):
            hr = slice(h * GLA_DK, (h + 1) * GLA_DK)
            hv = slice(h * GLA_DV, (h + 1) * GLA_DV)
            outs.append(o_inter[hr] + jnp.dot(attb[hr], v[:, hv], preferred_element_type=F32))
            news.append(upd[hr, hv])
        o_ref[rows, :] = jnp.concatenate(outs, axis=1)
        s_ref[...] = dec * s_old + jnp.concatenate(news, axis=0)

    @pl.when(j == pl.num_programs(1) - 1)
    def _():
        sout_ref[0] = s_ref[...]


def _gla_prompt(gq, gk, gv, misc2d, w2, ba, n, s, tc):
    nj = s // tc
    mall, lmask, hmask = _gla_consts()
    row = lambda w: pl.BlockSpec((tc, w), lambda b, j: (b * nj + j, 0))
    return pl.pallas_call(
        functools.partial(_gla_kernel, nchunk=tc // GLA_CHUNK),
        grid=(n, nj),
        in_specs=[row(GLA_KW), row(GLA_KW), row(GLA_WIDTH), row(MISC_W),
                  _const_spec(w2.shape), _const_spec(ba.shape), _const_spec(mall.shape),
                  _const_spec(lmask.shape), _const_spec(hmask.shape)],
        out_specs=[row(GLA_WIDTH), pl.BlockSpec((1, GLA_KW, GLA_DV), lambda b, j: (b, 0, 0))],
        out_shape=[jax.ShapeDtypeStruct((n * s, GLA_WIDTH), F32),
                   jax.ShapeDtypeStruct((n, GLA_KW, GLA_DV), F32)],
        scratch_shapes=[pltpu.VMEM((GLA_KW, GLA_DV), F32)],
        compiler_params=_cparams(("parallel", "arbitrary")),
        name="gla_prompt",
    )(gq, gk, gv, misc2d, w2, ba, mall, lmask, hmask)


def _softmax_rows(s_parts, s_new):
    m = s_new
    for s in s_parts:
        m = jnp.maximum(m, jnp.max(s, axis=1, keepdims=True))
    p_parts = [jnp.exp(s - m) for s in s_parts]
    p_new = jnp.exp(s_new - m)
    l = p_new
    for p in p_parts:
        l = l + jnp.sum(p, axis=1, keepdims=True)
    return p_parts, p_new, 1.0 / jnp.maximum(l, 1e-30)


def _bf16_round(x):
    return x.astype(BF16).astype(F32)


def _nsa_sample_kernel(pt_ref, page_ref, qbd_ref, gts_ref, kvn_ref, wkvn_ref, win_ref, w2_ref,
                       pe_ref, o_ref, xck_ref, xcv_ref, xsk_ref, xsv_ref, *, npages, page, past):
    del pt_ref
    p = pl.program_id(1)
    rows = pl.ds(pl.multiple_of(p * page, page), page)
    for c, ref in enumerate((xck_ref, xcv_ref, xsk_ref, xsv_ref)):
        ref[rows, :] = page_ref[0, :, c * NSA_KVW:(c + 1) * NSA_KVW]

    @pl.when(p == npages - 1)
    def _():
        nb = past // BLOCK
        nbs = nb + 1
        pos = past
        cur = pos // BLOCK
        nheads = NSA_HEADS
        dims_t = (((1,), (1,)), ((), ()))
        qf = _bf16_round(qbd_ref[0] * (NSA_HD ** -0.5))
        qb = qf.astype(BF16)
        kvn = kvn_ref[0]
        wkvn = wkvn_ref[0]

        ck, cv = _compress(xck_ref, xcv_ref, w2_ref, pe_ref, nb)
        sc = lax.dot_general(qb, ck.astype(BF16), dims_t, preferred_element_type=F32)
        bl = lax.broadcasted_iota(jnp.int32, (nheads, nb), 1)
        valid = (bl + 1) * BLOCK - 1 <= pos
        sc = jnp.where(valid, sc, NEG)
        mc = jnp.max(sc, axis=1, keepdims=True)
        pc = jnp.where(valid, jnp.exp(sc - mc), 0.0)
        pc = pc / jnp.maximum(jnp.sum(pc, axis=1, keepdims=True), 1e-30)
        oc = jnp.dot(pc.astype(BF16), cv.astype(BF16), preferred_element_type=F32)

        imp = jnp.concatenate(
            [jnp.broadcast_to(jnp.sum(pc[g * NSA_HPG:(g + 1) * NSA_HPG], axis=0, keepdims=True),
                              (NSA_HPG, nb)) for g in range(NSA_KV_HEADS)], axis=0)
        width = 2 * nb
        imp = jnp.concatenate([imp, jnp.zeros((nheads, width - nb), F32)], axis=1)
        bi = lax.broadcasted_iota(jnp.int32, (nheads, width), 1)
        bf = bi.astype(F32)
        causal = bi <= cur
        forced = jnp.logical_and(causal, jnp.logical_or(bi == 0, bi > cur - N_LOCAL))
        score = jnp.where(forced, FORCE_SCORE, jnp.where(causal, imp, -1.0))
        score = jnp.where(bi < nbs, score, -3.0)

        def pick_one(_, c):
            score, sel = c
            mx = jnp.max(score, axis=1, keepdims=True)
            idx = jnp.min(jnp.where(score == mx, bf, float(width)), axis=1, keepdims=True)
            pick = bf == idx
            return jnp.where(pick, -4.0, score), jnp.where(pick, 1.0, sel)

        _, sel = lax.fori_loop(0, min(N_SELECT, nbs), pick_one,
                               (score, jnp.zeros((nheads, width), F32)))
        selb = jnp.where(sel > 0.0, 0.0, NEG)
        selb_past = selb[:, :nb]
        selb_new = selb[:, nb:nb + 1]

        s_parts = []
        for l in range(BLOCK):
            kl = xsk_ref[pl.ds(l, nb, stride=BLOCK), :].astype(BF16)
            s_parts.append(lax.dot_general(qb, kl, dims_t, preferred_element_type=F32) + selb_past)
        s_new = jnp.sum(qf * _bf16_round(kvn[:, 2 * NSA_KVW:3 * NSA_KVW]), axis=1,
                        keepdims=True) + selb_new
        p_parts, p_new, inv = _softmax_rows(s_parts, s_new)
        acc = _bf16_round(p_new) * _bf16_round(kvn[:, 3 * NSA_KVW:])
        for l in range(BLOCK):
            vl = xsv_ref[pl.ds(l, nb, stride=BLOCK), :].astype(BF16)
            acc += jnp.dot(p_parts[l].astype(BF16), vl, preferred_element_type=F32)
        os_ = acc * inv

        win = win_ref[0]
        wlen = win.shape[0]
        sw = lax.dot_general(qb, win[:, :NSA_KVW].astype(BF16), dims_t,
                             preferred_element_type=F32)
        wpos = past - wlen + lax.broadcasted_iota(jnp.int32, (nheads, wlen), 1)
        dist = pos - wpos
        sw = jnp.where(jnp.logical_and(jnp.logical_and(dist >= 0, dist <= WINDOW), wpos >= 0),
                       sw, NEG)
        sw_new = jnp.sum(qf * _bf16_round(wkvn[:, :NSA_KVW]), axis=1, keepdims=True)
        (pw,), pw_new, inv_w = _softmax_rows([sw], sw_new)
        ow = (jnp.dot(pw.astype(BF16), win[:, NSA_KVW:].astype(BF16), preferred_element_type=F32)
              + _bf16_round(pw_new) * _bf16_round(wkvn[:, NSA_KVW:])) * inv_w

        gt = _sigmoid(gts_ref[0])
        o_ref[0] = gt[:, 0:1] * oc + gt[:, 1:2] * os_ + gt[:, 2:3] * ow


def _nsa_sample(page_table, cache3, qbd, gts3, kvn3, wkvn3, win3, w2, pe):
    nd, npages = page_table.shape
    page = cache3.shape[1]
    past = npages * page
    wlen = win3.shape[1]
    grid_spec = pltpu.PrefetchScalarGridSpec(
        num_scalar_prefetch=1,
        grid=(nd, npages),
        in_specs=[pl.BlockSpec((1, page, 4 * NSA_KVW), lambda b, p, pt: (pt[b * npages + p], 0, 0)),
                  pl.BlockSpec((1, NSA_HEADS, NSA_KVW), lambda b, p, pt: (b, 0, 0)),
                  pl.BlockSpec((1, NSA_HEADS, NSA_KVW), lambda b, p, pt: (b, 0, 0)),
                  pl.BlockSpec((1, 1, 4 * NSA_KVW), lambda b, p, pt: (b, 0, 0)),
                  pl.BlockSpec((1, 1, 2 * NSA_KVW), lambda b, p, pt: (b, 0, 0)),
                  pl.BlockSpec((1, wlen, 2 * NSA_KVW), lambda b, p, pt: (b, 0, 0)),
                  pl.BlockSpec(w2.shape, lambda b, p, pt: (0, 0, 0, 0),
                               pipeline_mode=pl.Buffered(1)),
                  pl.BlockSpec(pe.shape, lambda b, p, pt: (0, 0, 0, 0),
                               pipeline_mode=pl.Buffered(1))],
        out_specs=pl.BlockSpec((1, NSA_HEADS, NSA_KVW), lambda b, p, pt: (b, 0, 0)),
        scratch_shapes=[pltpu.VMEM((past, NSA_KVW), F32)] * 4)
    return pl.pallas_call(
        functools.partial(_nsa_sample_kernel, npages=npages, page=page, past=past),
        grid_spec=grid_spec,
        out_shape=jax.ShapeDtypeStruct((nd, NSA_HEADS, NSA_KVW), F32),
        compiler_params=_cparams(("parallel", "arbitrary")),
        name="nsa_sample",
    )(page_table.reshape(-1), cache3, qbd, gts3, kvn3, wkvn3, win3, w2, pe)


def _gla_sample_kernel(gq_ref, gk_ref, gv_ref, misc_ref, w2_ref, ba_ref, s_ref, o_ref, sout_ref,
                       *, ns):
    la = _log_decay(misc_ref[...], w2_ref, ba_ref)
    pad = jnp.zeros((128 - 3 * ns, GLA_KW), F32)
    zt = jnp.concatenate([gq_ref[...] * (GLA_DK ** -0.5), gk_ref[...], jnp.exp(la), pad],
                         axis=0).T
    v = gv_ref[...]
    for i in range(ns):
        outs = []
        for h in range(GLA_HEADS):
            hr = slice(h * GLA_DK, (h + 1) * GLA_DK)
            qc = zt[hr, i:i + 1]
            kc = zt[hr, ns + i:ns + i + 1]
            ac = zt[hr, 2 * ns + i:2 * ns + i + 1]
            s_new = ac * s_ref[i, hr, :] + kc * v[i:i + 1, h * GLA_DV:(h + 1) * GLA_DV]
            sout_ref[i, hr, :] = s_new
            outs.append(jnp.sum(qc * s_new, axis=0, keepdims=True))
        o_ref[i:i + 1, :] = jnp.concatenate(outs, axis=1)


def _gla_sample(gq, gk, gv, misc2d, w2, ba, state3, ns=32):
    nd = gq.shape[0]
    row = lambda w: pl.BlockSpec((ns, w), lambda i: (i, 0))
    st = pl.BlockSpec((ns, GLA_KW, GLA_DV), lambda i: (i, 0, 0))
    return pl.pallas_call(
        functools.partial(_gla_sample_kernel, ns=ns),
        grid=(nd // ns,),
        in_specs=[row(GLA_KW), row(GLA_KW), row(GLA_WIDTH), row(MISC_W),
                  _const_spec(w2.shape), _const_spec(ba.shape), st],
        out_specs=[row(GLA_WIDTH), st],
        out_shape=[jax.ShapeDtypeStruct((nd, GLA_WIDTH), F32),
                   jax.ShapeDtypeStruct(state3.shape, F32)],
        compiler_params=_cparams(("parallel",)),
        name="gla_sample",
    )(gq, gk, gv, misc2d, w2, ba, state3)


def _win_shift_kernel(win_ref, new_ref, o_ref, *, ns):
    wlen = win_ref.shape[1]
    last = lax.broadcasted_iota(jnp.int32, win_ref.shape[1:], 0) == wlen - 1
    for i in range(ns):
        o_ref[i] = jnp.where(last, new_ref[i], pltpu.roll(win_ref[i], wlen - 1, axis=0))


def _win_shift(win3, new3, ns=8):
    nd, wlen, w = win3.shape
    return pl.pallas_call(
        functools.partial(_win_shift_kernel, ns=ns),
        grid=(nd // ns,),
        in_specs=[pl.BlockSpec((ns, wlen, w), lambda i: (i, 0, 0)),
                  pl.BlockSpec((ns, 1, w), lambda i: (i, 0, 0))],
        out_specs=pl.BlockSpec((ns, wlen, w), lambda i: (i, 0, 0)),
        out_shape=jax.ShapeDtypeStruct(win3.shape, F32),
        compiler_params=_cparams(("parallel",)),
        name="win_shift",
    )(win3, new3)


def kernel(x_prompt, x_sample, cache_kv, cache_win, state_gla, page_table, p_prompt, p_sample,
           g_attn, w_in, w_cmp, pe_cmp, g_nsa_out, w_gla_a2, b_gla_a, g_gla_out, w_out, g_ffn,
           w_gate_up, w_down, g_ple, w_ple_gate, w_ple_proj, g_final):
    assert g_attn.shape[0] == 1, "single-layer trunk"
    n, s, d = x_prompt.shape
    nd, ds, _ = x_sample.shape
    assert ds == 1 and s % Q_BLOCK == 0 and s >= WINDOW
    w1 = _reorder_w_in(w_in[0])
    fw = _ffn_weights(g_nsa_out[0], g_gla_out[0], w_out[0], g_ffn[0], w_gate_up[0], w_down[0],
                      g_ple[0], w_ple_gate[0], w_ple_proj[0], g_final)
    cw2, cpe = _cmp_weights(w_cmp[0], pe_cmp[0])
    gw2, gba = _gla_gate_weights(w_gla_a2[0], b_gla_a[0])

    xp = x_prompt.reshape(n * s, d)
    q, kv, wkv, gq, gk, gv, gr, misc = _project(xp, g_attn[0], w1, min(512, n * s))
    cmat, ksl, vst, kw, vwt = _nsa_prep(kv, wkv, cw2, cpe, n, s, min(4096, s))
    o_n = _nsa_prompt(q, misc, cmat, ksl, vst, kw, vwt, n, s)
    o_g, s_p = _gla_prompt(gq, gk, gv, misc, gw2, gba, n, s, min(512, s))
    y_p = _mixer_ffn(o_n, o_g, gr, xp, p_prompt[0].reshape(n * s, -1), fw, min(256, n * s))

    xs = x_sample.reshape(nd, d)
    qs, kvs, wkvs, gqs, gks, gvs, grs, miscs = _project(xs, g_attn[0], w1, nd)
    q4 = qs.reshape(nd, NSA_KV_HEADS, NSA_HPG, NSA_HD)
    z4 = jnp.zeros_like(q4[:, 0])
    qbd = jnp.concatenate([jnp.concatenate([q4[:, 0], z4], axis=-1),
                           jnp.concatenate([z4, q4[:, 1]], axis=-1)], axis=1)
    gts3 = jnp.pad(miscs[:, :N_GATES].reshape(nd, NSA_HEADS, 3), ((0, 0), (0, 0), (0, NSA_KVW - 3)))
    cache3 = cache_kv[0].reshape(cache_kv.shape[1], cache_kv.shape[2], 4 * NSA_KVW)
    win3 = cache_win[0].reshape(nd, cache_win.shape[2], 2 * NSA_KVW)
    o8 = _nsa_sample(page_table, cache3, qbd, gts3, kvs.reshape(nd, 1, -1),
                     wkvs.reshape(nd, 1, -1), win3, cw2, cpe)
    o8 = o8.reshape(nd, NSA_KV_HEADS, NSA_HPG, NSA_KV_HEADS, NSA_HD)
    o_ns = jnp.stack([o8[:, 0, :, 0], o8[:, 1, :, 1]], axis=1).reshape(nd, NSA_WIDTH)
    o_gs, s_s = _gla_sample(gqs, gks, gvs, miscs, gw2, gba,
                            state_gla[0].reshape(nd, GLA_KW, GLA_DV))
    y_s = _mixer_ffn(o_ns, o_gs, grs, xs, p_sample[0].reshape(nd, -1), fw, nd)
    win_new = _win_shift(win3, wkvs.reshape(nd, 1, -1))

    wkeep = min(WINDOW, s)
    kvshape = (4, NSA_KV_HEADS, NSA_HD)
    wshape = (2, NSA_KV_HEADS, NSA_HD)
    return (y_p.reshape(n, s, d),
            y_s.reshape(nd, 1, d),
            kv.reshape((1, n, s) + kvshape),
            wkv.reshape(n, s, -1)[:, s - wkeep:].reshape((1, n, wkeep) + wshape),
            s_p.reshape(1, n, GLA_HEADS, GLA_DK, GLA_DV),
            kvs.reshape((1, nd, 1) + kvshape),
            win_new.reshape((1, nd, win3.shape[1]) + wshape),
            s_s.reshape(1, nd, GLA_HEADS, GLA_DK, GLA_DV))
```

```python
import functools

import numpy as np
import jax
import jax.numpy as jnp
from jax import lax
from jax.experimental import pallas as pl
from jax.experimental.pallas import tpu as pltpu

F32 = jnp.float32
BF16 = jnp.bfloat16

EPS = 1e-6
NEG = -1e30

NSA_HEADS = 8
NSA_KV_HEADS = 2
NSA_HPG = NSA_HEADS // NSA_KV_HEADS
NSA_HD = 64
NSA_WIDTH = NSA_HEADS * NSA_HD
NSA_KVW = NSA_KV_HEADS * NSA_HD
BLOCK = 64
N_SELECT = 16
N_LOCAL = 2
WINDOW = 512
Q_BLOCK = 128
FORCE_SCORE = float(NSA_HPG + 1)
GLA_HEADS = 4
GLA_DK = 64
GLA_DV = 128
GLA_WIDTH = GLA_HEADS * GLA_DV
GLA_KW = GLA_HEADS * GLA_DK
GLA_GATE_RANK = 16
GLA_GATE_TEMP = 16.0
GLA_CHUNK = 64
N_GATES = 3 * NSA_HEADS
MISC_W = 128

VMEM_LIMIT = 56 * 1024 * 1024


def _cparams(sem):
    return pltpu.CompilerParams(dimension_semantics=sem, vmem_limit_bytes=VMEM_LIMIT)


def _const_spec(shape):
    nd = len(shape)
    return pl.BlockSpec(shape, lambda *_: (0,) * nd, pipeline_mode=pl.Buffered(1))


def _sigmoid(x):
    return 1.0 / (1.0 + jnp.exp(-x))


def _rms(x, g):
    ms = jnp.mean(x * x, axis=-1, keepdims=True)
    return x * lax.rsqrt(ms + EPS) * g


def _split3(x):
    hi = x.astype(BF16)
    r = x - hi.astype(F32)
    mid = r.astype(BF16)
    lo = (r - mid.astype(F32)).astype(BF16)
    return hi, mid, lo


_P_Q, _P_KV, _P_WKV, _P_GQ, _P_GK, _P_GV, _P_GR, _P_MISC, _P_END = (
    0, 512, 1024, 1280, 1536, 1792, 2304, 2816, 2944)


def _reorder_w_in(w_in):
    offs = np.cumsum([0, NSA_WIDTH, 4 * NSA_KVW, 2 * NSA_KVW, N_GATES, GLA_KW, GLA_KW,
                      GLA_WIDTH, GLA_GATE_RANK, GLA_WIDTH])
    q, kv, wkv, gts, gq, gk, gv, ga, gr = [w_in[:, offs[i]:offs[i + 1]] for i in range(9)]
    pad = jnp.zeros((w_in.shape[0], MISC_W - N_GATES - GLA_GATE_RANK), w_in.dtype)
    return jnp.concatenate([q, kv, wkv, gq, gk, gv, gr, gts, ga, pad], axis=1).astype(BF16)


def _proj_kernel(x_ref, g_ref, w_ref, wt_ref, q_ref, kv_ref, wkv_ref, gq_ref, gk_ref, gv_ref,
                 gr_ref, misc_ref, kvt_ref, wkvt_ref):
    xn = _rms(x_ref[...], g_ref[...]).astype(BF16)
    for ref, a, b in ((q_ref, _P_Q, _P_KV), (kv_ref, _P_KV, _P_WKV), (wkv_ref, _P_WKV, _P_GQ),
                      (gq_ref, _P_GQ, _P_GK), (gk_ref, _P_GK, _P_GV), (gv_ref, _P_GV, _P_GR),
                      (gr_ref, _P_GR, _P_MISC), (misc_ref, _P_MISC, _P_END)):
        ref[...] = jnp.dot(xn, w_ref[:, a:b], preferred_element_type=F32)
    dims_t = (((1,), (1,)), ((), ()))
    nkv = kvt_ref.shape[1]
    kvt_ref[0] = lax.dot_general(wt_ref[0:nkv, :], xn, dims_t, preferred_element_type=F32)
    wkvt_ref[0] = lax.dot_general(wt_ref[nkv:, :], xn, dims_t, preferred_element_type=F32)


def _project(x2d, g_attn, w1, n, tm):
    t, d = x2d.shape
    s = t // n
    nj = s // tm
    widths = (512, 512, 256, 256, 256, 512, 512, MISC_W)
    wt = w1[:, _P_KV:_P_GQ].T
    tspec = lambda rows: pl.BlockSpec((1, rows, tm), lambda i: (i // nj, 0, i % nj))
    return pl.pallas_call(
        _proj_kernel,
        grid=(t // tm,),
        in_specs=[pl.BlockSpec((tm, d), lambda i: (i, 0)),
                  _const_spec((1, d)),
                  _const_spec(w1.shape),
                  _const_spec(wt.shape)],
        out_specs=[pl.BlockSpec((tm, w), lambda i: (i, 0)) for w in widths]
                  + [tspec(4 * NSA_KVW), tspec(2 * NSA_KVW)],
        out_shape=[jax.ShapeDtypeStruct((t, w), F32) for w in widths]
                  + [jax.ShapeDtypeStruct((n, 4 * NSA_KVW, s), F32),
                     jax.ShapeDtypeStruct((n, 2 * NSA_KVW, s), F32)],
        compiler_params=_cparams(("parallel",)),
        name="proj",
    )(x2d, g_attn.reshape(1, d), w1, wt)


def _group_mean_sq(x, bd_ref, width):
    hi, mid, lo = _split3(x * x)
    bd = bd_ref[...]
    s = (jnp.dot(hi, bd, preferred_element_type=F32) + jnp.dot(mid, bd, preferred_element_type=F32)
         + jnp.dot(lo, bd, preferred_element_type=F32))
    return s * (1.0 / width)


def _ffn_kernel(on_ref, og_ref, gr_ref, x_ref, p_ref, gn_ref, gg_ref, bd64_ref, bd128_ref,
                wo_ref, gffn_ref, wgu_ref, wd_ref, gple_ref, wpg_ref, wpp_ref, gfin_ref,
                y_ref, h_ref, acc_ref, *, n_ff):
    o_n = on_ref[...]
    o_g = og_ref[...]
    r = gr_ref[...]
    a_n = o_n * lax.rsqrt(_group_mean_sq(o_n, bd64_ref, NSA_HD) + EPS) * gn_ref[...]
    a_g = o_g * lax.rsqrt(_group_mean_sq(o_g, bd128_ref, GLA_DV) + EPS) * gg_ref[...]
    a_g = a_g * (r * _sigmoid(r))
    x1 = (x_ref[...]
          + jnp.dot(a_n.astype(BF16), wo_ref[0:NSA_WIDTH, :], preferred_element_type=F32)
          + jnp.dot(a_g.astype(BF16), wo_ref[NSA_WIDTH:, :], preferred_element_type=F32))
    h_ref[...] = _rms(x1, gffn_ref[...]).astype(BF16)
    acc_ref[...] = x1

    def body(c, carry):
        gu = jnp.dot(h_ref[...], wgu_ref[c], preferred_element_type=F32)
        half = gu.shape[1] // 2
        gt = gu[:, :half]
        act = (gt * _sigmoid(gt) * gu[:, half:]).astype(BF16)
        acc_ref[...] += jnp.dot(act, wd_ref[c], preferred_element_type=F32)
        return carry

    lax.fori_loop(0, n_ff, body, 0)
    x2 = acc_ref[...]
    gate = _sigmoid(jnp.dot(_rms(x2, gple_ref[...]).astype(BF16), wpg_ref[...],
                            preferred_element_type=F32))
    x3 = x2 + jnp.dot(p_ref[...].astype(BF16), wpp_ref[...], preferred_element_type=F32) * gate
    y_ref[...] = _rms(x3, gfin_ref[...])


def _block_diag_ones(n, width):
    idx = np.arange(n) // width
    return jnp.asarray((idx[:, None] == idx[None, :]).astype(np.float32), BF16)


def _ffn_weights(g_nsa_out, g_gla_out, w_out, g_ffn, w_gate_up, w_down, g_ple, w_ple_gate,
                 w_ple_proj, g_final, ff_chunk=256):
    d = w_out.shape[1]
    d_ff = w_down.shape[0]
    n_ff = d_ff // ff_chunk
    wg = w_gate_up[:, :d_ff].reshape(d, n_ff, ff_chunk)
    wu = w_gate_up[:, d_ff:].reshape(d, n_ff, ff_chunk)
    wgu = jnp.concatenate([wg, wu], axis=2).transpose(1, 0, 2).astype(BF16)
    wd = w_down.reshape(n_ff, ff_chunk, d).astype(BF16)
    return dict(gn=g_nsa_out.reshape(1, NSA_WIDTH), gg=g_gla_out.reshape(1, GLA_WIDTH),
                bd64=_block_diag_ones(NSA_WIDTH, NSA_HD), bd128=_block_diag_ones(GLA_WIDTH, GLA_DV),
                wo=w_out.astype(BF16), gffn=g_ffn.reshape(1, d), wgu=wgu, wd=wd,
                gple=g_ple.reshape(1, d), wpg=w_ple_gate.astype(BF16),
                wpp=w_ple_proj.astype(BF16), gfin=g_final.reshape(1, d))


def _mixer_ffn(o_n, o_g, gr, x2d, p2d, fw, tm):
    t, d = x2d.shape
    n_ff = fw["wgu"].shape[0]
    consts = [fw[k] for k in ("gn", "gg", "bd64", "bd128", "wo", "gffn", "wgu", "wd", "gple",
                              "wpg", "wpp", "gfin")]
    row = lambda w: pl.BlockSpec((tm, w), lambda i: (i, 0))
    return pl.pallas_call(
        functools.partial(_ffn_kernel, n_ff=n_ff),
        grid=(t // tm,),
        in_specs=[row(NSA_WIDTH), row(GLA_WIDTH), row(GLA_WIDTH), row(d), row(p2d.shape[1])]
                 + [_const_spec(c.shape) for c in consts],
        out_specs=row(d),
        out_shape=jax.ShapeDtypeStruct((t, d), F32),
        scratch_shapes=[pltpu.VMEM((tm, d), BF16), pltpu.VMEM((tm, d), F32)],
        compiler_params=_cparams(("parallel",)),
        name="mixer_ffn",
    )(o_n, o_g, gr, x2d, p2d, *consts)


def _cmp_weights(w_cmp, pe_cmp):
    z = jnp.zeros((BLOCK, NSA_HD, NSA_HD), w_cmp.dtype)

    def bd(w):
        return jnp.concatenate([jnp.concatenate([w, z], axis=2),
                                jnp.concatenate([z, w], axis=2)], axis=1)

    w2 = jnp.stack([bd(w_cmp[0]), bd(w_cmp[1])], axis=1).astype(BF16)
    pe = jnp.stack([jnp.concatenate([pe_cmp[0], pe_cmp[0]], axis=1),
                    jnp.concatenate([pe_cmp[1], pe_cmp[1]], axis=1)], axis=1)
    return w2, pe.reshape(BLOCK, 2, 1, NSA_KVW)


def _compress(xk_ref, xv_ref, w2_ref, pe_ref, nb):
    acc_k = jnp.zeros((nb, NSA_KVW), F32)
    acc_v = jnp.zeros((nb, NSA_KVW), F32)
    for l in range(BLOCK):
        xk = (xk_ref[pl.ds(l, nb, stride=BLOCK), :] + pe_ref[l, 0]).astype(BF16)
        xv = (xv_ref[pl.ds(l, nb, stride=BLOCK), :] + pe_ref[l, 1]).astype(BF16)
        acc_k += jnp.dot(xk, w2_ref[l, 0], preferred_element_type=F32)
        acc_v += jnp.dot(xv, w2_ref[l, 1], preferred_element_type=F32)
    return acc_k, acc_v


def _prep_kernel(ckin_ref, cvin_ref, ksin_ref, kwin_ref, vst_in_ref, vwt_in_ref, w2_ref, pe_ref,
                 c_ref, ks_ref, vst_ref, kw_ref, vwt_ref, *, tk):
    nb = tk // BLOCK
    ck, cv = _compress(ckin_ref, cvin_ref, w2_ref, pe_ref, nb)
    c_ref[:, 0:NSA_KVW] = ck
    c_ref[:, NSA_KVW:] = cv
    ks_ref[...] = ksin_ref[...].astype(BF16)
    kw_ref[...] = kwin_ref[...].astype(BF16)
    vst_ref[...] = vst_in_ref[...].astype(BF16)
    vwt_ref[...] = vwt_in_ref[...].astype(BF16)


def _nsa_prep(kv2d, wkv2d, kvt, wkvt, w2, pe, n, s, tk):
    t = n * s
    nj = s // tk
    kv_specs = [pl.BlockSpec((tk, NSA_KVW), (lambda b, j, c=c: (b * nj + j, c))) for c in range(3)]
    return pl.pallas_call(
        functools.partial(_prep_kernel, tk=tk),
        grid=(n, nj),
        in_specs=kv_specs + [pl.BlockSpec((tk, NSA_KVW), lambda b, j: (b * nj + j, 0)),
                             pl.BlockSpec((1, NSA_KVW, tk), lambda b, j: (b, 3, j)),
                             pl.BlockSpec((1, NSA_KVW, tk), lambda b, j: (b, 1, j)),
                             _const_spec(w2.shape), _const_spec(pe.shape)],
        out_specs=[pl.BlockSpec((tk // BLOCK, 2 * NSA_KVW), lambda b, j: (b * nj + j, 0)),
                   pl.BlockSpec((tk, NSA_KVW), lambda b, j: (b * nj + j, 0)),
                   pl.BlockSpec((1, NSA_KVW, tk), lambda b, j: (b, 0, j)),
                   pl.BlockSpec((tk, NSA_KVW), lambda b, j: (b * nj + j, 0)),
                   pl.BlockSpec((1, NSA_KVW, tk), lambda b, j: (b, 0, j))],
        out_shape=[jax.ShapeDtypeStruct((t // BLOCK, 2 * NSA_KVW), F32),
                   jax.ShapeDtypeStruct((t, NSA_KVW), BF16),
                   jax.ShapeDtypeStruct((n, NSA_KVW, s), BF16),
                   jax.ShapeDtypeStruct((t, NSA_KVW), BF16),
                   jax.ShapeDtypeStruct((n, NSA_KVW, s), BF16)],
        compiler_params=_cparams(("parallel", "parallel")),
        name="nsa_prep",
    )(kv2d, kv2d, kv2d, wkv2d, kvt, wkvt, w2, pe)


_LQ = NSA_KV_HEADS * NSA_HPG * Q_BLOCK
_LG = NSA_HPG * Q_BLOCK
_SEL_GROUP = 4 * Q_BLOCK


def _rep_heads(x):
    a, b = x[:, :Q_BLOCK], x[:, Q_BLOCK:]
    return jnp.concatenate([a] * NSA_HPG + [b] * NSA_HPG, axis=1)


def _flash_tile(k, vt, qt, carry, mask=None):
    s = jnp.dot(k, qt, preferred_element_type=F32)
    if mask is not None:
        s = jnp.where(mask, s, NEG)
    return _flash_update(s, vt, carry)


def _flash_update(s, vt, carry):
    m, l, acc0, acc1 = carry
    m_new = jnp.maximum(m, jnp.max(s, axis=0, keepdims=True))
    alpha = jnp.exp(m - m_new)
    p = jnp.exp(s - m_new)
    l = alpha * l + jnp.sum(p, axis=0, keepdims=True)
    pb = p.astype(BF16)
    acc0 = alpha[:, :_LG] * acc0 + jnp.dot(vt[:NSA_HD], pb[:, :_LG], preferred_element_type=F32)
    acc1 = alpha[:, _LG:] * acc1 + jnp.dot(vt[NSA_HD:], pb[:, _LG:], preferred_element_type=F32)
    return m_new, l, acc0, acc1


def _flash_init():
    return (jnp.full((1, _LQ), NEG, F32), jnp.zeros((1, _LQ), F32),
            jnp.zeros((NSA_HD, _LG), F32), jnp.zeros((NSA_HD, _LG), F32))


def _flash_out(carry):
    m, l, acc0, acc1 = carry
    inv = 1.0 / jnp.maximum(l, 1e-30)
    return acc0 * inv[:, :_LG], acc1 * inv[:, _LG:]


def _select_blocks(imp, cur, nbs):
    bi = lax.broadcasted_iota(jnp.int32, imp.shape, 0)
    bf = bi.astype(F32)
    causal = bi <= cur
    forced = jnp.logical_and(causal, jnp.logical_or(bi == 0, bi > cur - N_LOCAL))
    score = jnp.where(forced, FORCE_SCORE, jnp.where(causal, imp, -1.0))

    def body(_, c):
        score, sel = c
        mx = jnp.max(score, axis=0, keepdims=True)
        idx = jnp.min(jnp.where(score == mx, bf, float(nbs)), axis=0, keepdims=True)
        pick = bf == idx
        return jnp.where(pick, -2.0, score), jnp.where(pick, 1.0, sel)

    _, sel = lax.fori_loop(0, min(N_SELECT, nbs), body, (score, jnp.zeros(imp.shape, F32)))
    return sel


def _nsa_prompt_kernel(q_ref, misc_ref, c_ref, ks_ref, vst_ref, ind_ref, *rest, nbc):
    kw_refs = rest[0:5]
    vwt_refs = rest[5:10]
    o_ref = rest[10]
    selb_ref = rest[11]
    i = pl.program_id(1)

    qT = (q_ref[...] * (NSA_HD ** -0.5)).T
    z = jnp.zeros((NSA_HD, Q_BLOCK), F32)
    top = [qT[NSA_HD * h:NSA_HD * (h + 1)] for h in range(NSA_HPG)] + [z] * NSA_HPG
    bot = [z] * NSA_HPG + [qT[NSA_HD * h:NSA_HD * (h + 1)] for h in range(NSA_HPG, NSA_HEADS)]
    qt = jnp.concatenate([jnp.concatenate(top, axis=1), jnp.concatenate(bot, axis=1)],
                         axis=0).astype(BF16)

    cmat = c_ref[...]
    ck = cmat[:, :NSA_KVW].astype(BF16)
    cv = cmat[:, NSA_KVW:].astype(BF16)
    sc = jnp.dot(ck, qt, preferred_element_type=F32)
    b_io = lax.broadcasted_iota(jnp.int32, (nbc, _LQ), 0)
    qoff = jnp.bitwise_and(lax.broadcasted_iota(jnp.int32, (nbc, _LQ), 1), Q_BLOCK - 1)
    valid = b_io * BLOCK + (BLOCK - 1) <= i * Q_BLOCK + qoff
    sc = jnp.where(valid, sc, NEG)
    mc = jnp.max(sc, axis=0, keepdims=True)
    pc = jnp.where(valid, jnp.exp(sc - mc), 0.0)
    pc = pc / jnp.maximum(jnp.sum(pc, axis=0, keepdims=True), 1e-30)
    oc_full = lax.dot_general(cv, pc.astype(BF16), (((0,), (0,)), ((), ())),
                              preferred_element_type=F32)
    oc = (oc_full[:NSA_HD, :_LG], oc_full[NSA_HD:, _LG:])

    qo = lax.broadcasted_iota(jnp.int32, (1, Q_BLOCK), 1)
    cur = 2 * i + (qo >= BLOCK).astype(jnp.int32)
    for g in range(NSA_KV_HEADS):
        imp = pc[:, g * _LG:g * _LG + Q_BLOCK]
        for hh in range(1, NSA_HPG):
            imp = imp + pc[:, g * _LG + hh * Q_BLOCK:g * _LG + (hh + 1) * Q_BLOCK]
        sel = _select_blocks(imp, cur, nbc)
        selb_ref[:, g * Q_BLOCK:(g + 1) * Q_BLOCK] = jnp.where(sel > 0.0, 0.0, NEG)

    nbg = _SEL_GROUP // BLOCK
    ind = ind_ref[...]
    zpad = jnp.zeros((NSA_KVW - 16, _LQ), BF16)

    def sel_group(jg, carry, causal):
        off = pl.multiple_of(jg * _SEL_GROUP, _SEL_GROUP)
        ka = jnp.concatenate([ks_ref[pl.ds(off, _SEL_GROUP), :], ind], axis=1)
        brow = selb_ref[pl.ds(pl.multiple_of(jg * nbg, nbg), nbg), :]
        baug = jnp.concatenate([_rep_heads(brow), jnp.zeros((16 - nbg, _LQ), F32)],
                               axis=0).astype(BF16)
        s = jnp.dot(ka, jnp.concatenate([qt, baug, zpad], axis=0), preferred_element_type=F32)
        if causal:
            rowg = lax.broadcasted_iota(jnp.int32, (_SEL_GROUP, _LQ), 0)
            qcg = jnp.bitwise_and(lax.broadcasted_iota(jnp.int32, (_SEL_GROUP, _LQ), 1),
                                  Q_BLOCK - 1)
            s = jnp.where(off + rowg <= i * Q_BLOCK + qcg, s, NEG)
        return _flash_update(s, vst_ref[0, :, pl.ds(off, _SEL_GROUP)], carry)

    last = lax.shift_right_logical(i, 2)
    carry = lax.fori_loop(0, last, lambda jg, c: sel_group(jg, c, False), _flash_init())
    carry = sel_group(last, carry, True)
    os_ = _flash_out(carry)

    row = lax.broadcasted_iota(jnp.int32, (Q_BLOCK, _LQ), 0)
    qcol = jnp.bitwise_and(lax.broadcasted_iota(jnp.int32, (Q_BLOCK, _LQ), 1), Q_BLOCK - 1)
    carry = _flash_tile(kw_refs[4][...], vwt_refs[4][0], qt, _flash_init(), mask=row <= qcol)
    for t in range(4):
        first_ok = jnp.where(i - 4 + t >= 0, 0, Q_BLOCK)
        mask = row >= (qcol + first_ok if t == 0 else first_ok)
        carry = _flash_tile(kw_refs[t][...], vwt_refs[t][0], qt, carry, mask=mask)
    ow = _flash_out(carry)

    gt = _sigmoid(misc_ref[...]).T
    outs = []
    for g in range(NSA_KV_HEADS):
        for hh in range(NSA_HPG):
            h = g * NSA_HPG + hh
            sl = slice(hh * Q_BLOCK, (hh + 1) * Q_BLOCK)
            outs.append(gt[3 * h:3 * h + 1] * oc[g][:, sl] + gt[3 * h + 1:3 * h + 2] * os_[g][:, sl]
                        + gt[3 * h + 2:3 * h + 3] * ow[g][:, sl])
    o_ref[...] = jnp.concatenate(outs, axis=0).T


def _nsa_prompt(q2d, misc2d, cmat, ks, vst, kw, vwt, n, s):
    assert s % _SEL_GROUP == 0
    nq = s // Q_BLOCK
    nbc = s // BLOCK
    blk = np.arange(_SEL_GROUP) // BLOCK
    ind = jnp.asarray((blk[:, None] == np.arange(NSA_KVW)[None, :]).astype(np.float32), BF16)
    wk_specs = [pl.BlockSpec((Q_BLOCK, NSA_KVW),
                             (lambda b, i, t=t: (b * nq + jnp.maximum(i - 4 + t, 0), 0)))
                for t in range(5)]
    wv_specs = [pl.BlockSpec((1, NSA_KVW, Q_BLOCK),
                             (lambda b, i, t=t: (b, 0, jnp.maximum(i - 4 + t, 0))))
                for t in range(5)]
    return pl.pallas_call(
        functools.partial(_nsa_prompt_kernel, nbc=nbc),
        grid=(n, nq),
        in_specs=[pl.BlockSpec((Q_BLOCK, NSA_WIDTH), lambda b, i: (b * nq + i, 0)),
                  pl.BlockSpec((Q_BLOCK, MISC_W), lambda b, i: (b * nq + i, 0)),
                  pl.BlockSpec((nbc, 2 * NSA_KVW), lambda b, i: (b, 0)),
                  pl.BlockSpec((s, NSA_KVW), lambda b, i: (b, 0)),
                  pl.BlockSpec((1, NSA_KVW, s), lambda b, i: (b, 0, 0)),
                  _const_spec(ind.shape)] + wk_specs + wv_specs,
        out_specs=pl.BlockSpec((Q_BLOCK, NSA_WIDTH), lambda b, i: (b * nq + i, 0)),
        out_shape=jax.ShapeDtypeStruct((n * s, NSA_WIDTH), F32),
        scratch_shapes=[pltpu.VMEM((nbc, 2 * Q_BLOCK), F32)],
        compiler_params=_cparams(("parallel", "arbitrary")),
        name="nsa_prompt",
    )(q2d, misc2d, cmat, ks, vst, ind, *([kw] * 5), *([vwt] * 5))


_N_LEVELS = int(np.log2(GLA_CHUNK))


def _gla_consts():
    c = GLA_CHUNK
    t = np.arange(c)
    mats = [t[None, :] <= t[:, None], t[None, :] > t[:, None]]
    masks = []
    w = c // 2
    while w >= 1:
        blk = t // (2 * w)
        mid = blk * 2 * w + w
        upper = t >= mid
        m = np.zeros((c, c), bool)
        for r in range(c):
            if upper[r]:
                m[r, mid[r]:r + 1] = True
            else:
                m[r, r + 1:mid[r]] = True
        mats.append(m)
        masks.append((blk[:, None] == blk[None, :]) & upper[:, None] & ~upper[None, :])
        w //= 2
    masks.append(np.eye(c, dtype=bool))
    mall = jnp.asarray(np.concatenate(mats, axis=0).astype(np.float32), BF16)
    lmask = jnp.asarray(np.tile(np.stack(masks).astype(np.float32), (1, GLA_HEADS, 1)))
    hd = np.arange(GLA_KW) // GLA_DK
    hmask = jnp.asarray((hd[:, None] == hd[None, :]).astype(np.float32))
    return mall, lmask, hmask


def _gla_gate_weights(w_a2, b_a):
    w2 = jnp.zeros((MISC_W, GLA_KW), F32).at[N_GATES:N_GATES + GLA_GATE_RANK].set(w_a2)
    return w2.astype(BF16), b_a.reshape(1, GLA_KW)


def _log_decay(misc, w2_ref, ba_ref):
    x = jnp.dot(misc.astype(BF16), w2_ref[...], preferred_element_type=F32) + ba_ref[...]
    return (jnp.minimum(x, 0.0) - jnp.log1p(jnp.exp(-jnp.abs(x)))) * (1.0 / GLA_GATE_TEMP)


def _gla_kernel(gq_ref, gk_ref, gv_ref, misc_ref, w2_ref, ba_ref, mall_ref, lmask_ref, hm_ref,
                o_ref, sout_ref, s_ref, *, nchunk):
    j = pl.program_id(1)

    @pl.when(j == 0)
    def _():
        s_ref[...] = jnp.zeros(s_ref.shape, F32)

    la_all = _log_decay(misc_ref[...], w2_ref, ba_ref)
    mall = mall_ref[...]
    hm = hm_ref[...]
    c = GLA_CHUNK
    dims_t = (((1,), (1,)), ((), ()))

    def stack(x):
        return (jnp.concatenate([x] * GLA_HEADS, axis=0) * hm).astype(BF16)

    for ci in range(nchunk):
        rows = slice(ci * c, (ci + 1) * c)
        q = gq_ref[rows, :] * (GLA_DK ** -0.5)
        k = gk_ref[rows, :]
        v = gv_ref[rows, :].astype(BF16)
        la = la_all[rows]
        hi, mid, lo = _split3(la)
        ex = jnp.exp(jnp.dot(mall, hi, preferred_element_type=F32)
                     + jnp.dot(mall, mid, preferred_element_type=F32)
                     + jnp.dot(mall, lo, preferred_element_type=F32))
        att = lax.dot_general(stack(q), k.astype(BF16), dims_t,
                              preferred_element_type=F32) * lmask_ref[_N_LEVELS]
        for lev in range(_N_LEVELS):
            e = ex[(2 + lev) * c:(3 + lev) * c]
            att += lax.dot_general(stack(q * e), (k * e).astype(BF16), dims_t,
                                   preferred_element_type=F32) * lmask_ref[lev]
        attb = att.astype(BF16)
        s_old = s_ref[...]
        o_inter = jnp.dot(stack(q * ex[0:c]), s_old.astype(BF16), preferred_element_type=F32)
        kd = (k * ex[c:2 * c]).astype(BF16)
        upd = lax.dot_general(kd, v, (((0,), (0,)), ((), ())), preferred_element_type=F32)
        dec = jnp.exp(jnp.sum(la.T, axis=1, keepdims=True))
        outs, news = [], []
        for h in range(GLA_HEADS):
            hr = slice(h * GLA_DK, (h + 1) * GLA_DK)
            hv = slice(h * GLA_DV, (h + 1) * GLA_DV)
            outs.append(o_inter[hr] + jnp.dot(attb[hr], v[:, hv], preferred_element_type=F32))
            news.append(upd[hr, hv])
        o_ref[rows, :] = jnp.concatenate(outs, axis=1)
        s_ref[...] = dec * s_old + jnp.concatenate(news, axis=0)

    @pl.when(j == pl.num_programs(1) - 1)
    def _():
        sout_ref[0] = s_ref[...]


def _gla_prompt(gq, gk, gv, misc2d, w2, ba, n, s, tc):
    nj = s // tc
    mall, lmask, hmask = _gla_consts()
    row = lambda w: pl.BlockSpec((tc, w), lambda b, j: (b * nj + j, 0))
    return pl.pallas_call(
        functools.partial(_gla_kernel, nchunk=tc // GLA_CHUNK),
        grid=(n, nj),
        in_specs=[row(GLA_KW), row(GLA_KW), row(GLA_WIDTH), row(MISC_W),
                  _const_spec(w2.shape), _const_spec(ba.shape), _const_spec(mall.shape),
                  _const_spec(lmask.shape), _const_spec(hmask.shape)],
        out_specs=[row(GLA_WIDTH), pl.BlockSpec((1, GLA_KW, GLA_DV), lambda b, j: (b, 0, 0))],
        out_shape=[jax.ShapeDtypeStruct((n * s, GLA_WIDTH), F32),
                   jax.ShapeDtypeStruct((n, GLA_KW, GLA_DV), F32)],
        scratch_shapes=[pltpu.VMEM((GLA_KW, GLA_DV), F32)],
        compiler_params=_cparams(("parallel", "arbitrary")),
        name="gla_prompt",
    )(gq, gk, gv, misc2d, w2, ba, mall, lmask, hmask)


def _softmax_rows(s_parts, s_new):
    m = s_new
    for s in s_parts:
        m = jnp.maximum(m, jnp.max(s, axis=1, keepdims=True))
    p_parts = [jnp.exp(s - m) for s in s_parts]
    p_new = jnp.exp(s_new - m)
    l = p_new
    for p in p_parts:
        l = l + jnp.sum(p, axis=1, keepdims=True)
    return p_parts, p_new, 1.0 / jnp.maximum(l, 1e-30)


def _bf16_round(x):
    return x.astype(BF16).astype(F32)


def _nsa_sample_kernel(pt_ref, page_ref, qbd_ref, gts_ref, kvn_ref, wkvn_ref, win_ref, w2_ref,
                       pe_ref, o_ref, xck_ref, xcv_ref, xsk_ref, xsv_ref, *, npages, page, past):
    del pt_ref
    p = pl.program_id(1)
    rows = pl.ds(pl.multiple_of(p * page, page), page)
    for c, ref in enumerate((xck_ref, xcv_ref, xsk_ref, xsv_ref)):
        ref[rows, :] = page_ref[0, :, c * NSA_KVW:(c + 1) * NSA_KVW]

    @pl.when(p == npages - 1)
    def _():
        nb = past // BLOCK
        nbs = nb + 1
        pos = past
        cur = pos // BLOCK
        nheads = NSA_HEADS
        dims_t = (((1,), (1,)), ((), ()))
        qf = _bf16_round(qbd_ref[0] * (NSA_HD ** -0.5))
        qb = qf.astype(BF16)
        kvn = kvn_ref[0]
        wkvn = wkvn_ref[0]

        ck, cv = _compress(xck_ref, xcv_ref, w2_ref, pe_ref, nb)
        sc = lax.dot_general(qb, ck.astype(BF16), dims_t, preferred_element_type=F32)
        bl = lax.broadcasted_iota(jnp.int32, (nheads, nb), 1)
        valid = (bl + 1) * BLOCK - 1 <= pos
        sc = jnp.where(valid, sc, NEG)
        mc = jnp.max(sc, axis=1, keepdims=True)
        pc = jnp.where(valid, jnp.exp(sc - mc), 0.0)
        pc = pc / jnp.maximum(jnp.sum(pc, axis=1, keepdims=True), 1e-30)
        oc = jnp.dot(pc.astype(BF16), cv.astype(BF16), preferred_element_type=F32)

        imp = jnp.concatenate(
            [jnp.broadcast_to(jnp.sum(pc[g * NSA_HPG:(g + 1) * NSA_HPG], axis=0, keepdims=True),
                              (NSA_HPG, nb)) for g in range(NSA_KV_HEADS)], axis=0)
        width = 2 * nb
        imp = jnp.concatenate([imp, jnp.zeros((nheads, width - nb), F32)], axis=1)
        bi = lax.broadcasted_iota(jnp.int32, (nheads, width), 1)
        bf = bi.astype(F32)
        causal = bi <= cur
        forced = jnp.logical_and(causal, jnp.logical_or(bi == 0, bi > cur - N_LOCAL))
        score = jnp.where(forced, FORCE_SCORE, jnp.where(causal, imp, -1.0))
        score = jnp.where(bi < nbs, score, -3.0)

        def pick_one(_, c):
            score, sel = c
            mx = jnp.max(score, axis=1, keepdims=True)
            idx = jnp.min(jnp.where(score == mx, bf, float(width)), axis=1, keepdims=True)
            pick = bf == idx
            return jnp.where(pick, -4.0, score), jnp.where(pick, 1.0, sel)

        _, sel = lax.fori_loop(0, min(N_SELECT, nbs), pick_one,
                               (score, jnp.zeros((nheads, width), F32)))
        selb = jnp.where(sel > 0.0, 0.0, NEG)
        selb_past = selb[:, :nb]
        selb_new = selb[:, nb:nb + 1]

        s_parts = []
        for l in range(BLOCK):
            kl = xsk_ref[pl.ds(l, nb, stride=BLOCK), :].astype(BF16)
            s_parts.append(lax.dot_general(qb, kl, dims_t, preferred_element_type=F32) + selb_past)
        s_new = jnp.sum(qf * _bf16_round(kvn[:, 2 * NSA_KVW:3 * NSA_KVW]), axis=1,
                        keepdims=True) + selb_new
        p_parts, p_new, inv = _softmax_rows(s_parts, s_new)
        acc = _bf16_round(p_new) * _bf16_round(kvn[:, 3 * NSA_KVW:])
        for l in range(BLOCK):
            vl = xsv_ref[pl.ds(l, nb, stride=BLOCK), :].astype(BF16)
            acc += jnp.dot(p_parts[l].astype(BF16), vl, preferred_element_type=F32)
        os_ = acc * inv

        win = win_ref[0]
        wlen = win.shape[0]
        sw = lax.dot_general(qb, win[:, :NSA_KVW].astype(BF16), dims_t,
                             preferred_element_type=F32)
        wpos = past - wlen + lax.broadcasted_iota(jnp.int32, (nheads, wlen), 1)
        dist = pos - wpos
        sw = jnp.where(jnp.logical_and(jnp.logical_and(dist >= 0, dist <= WINDOW), wpos >= 0),
                       sw, NEG)
        sw_new = jnp.sum(qf * _bf16_round(wkvn[:, :NSA_KVW]), axis=1, keepdims=True)
        (pw,), pw_new, inv_w = _softmax_rows([sw], sw_new)
        ow = (jnp.dot(pw.astype(BF16), win[:, NSA_KVW:].astype(BF16), preferred_element_type=F32)
              + _bf16_round(pw_new) * _bf16_round(wkvn[:, NSA_KVW:])) * inv_w

        gt = _sigmoid(gts_ref[0])
        o_ref[0] = gt[:, 0:1] * oc + gt[:, 1:2] * os_ + gt[:, 2:3] * ow


def _nsa_sample(page_table, cache3, qbd, gts3, kvn3, wkvn3, win3, w2, pe):
    nd, npages = page_table.shape
    page = cache3.shape[1]
    past = npages * page
    wlen = win3.shape[1]
    grid_spec = pltpu.PrefetchScalarGridSpec(
        num_scalar_prefetch=1,
        grid=(nd, npages),
        in_specs=[pl.BlockSpec((1, page, 4 * NSA_KVW), lambda b, p, pt: (pt[b * npages + p], 0, 0)),
                  pl.BlockSpec((1, NSA_HEADS, NSA_KVW), lambda b, p, pt: (b, 0, 0)),
                  pl.BlockSpec((1, NSA_HEADS, NSA_KVW), lambda b, p, pt: (b, 0, 0)),
                  pl.BlockSpec((1, 1, 4 * NSA_KVW), lambda b, p, pt: (b, 0, 0)),
                  pl.BlockSpec((1, 1, 2 * NSA_KVW), lambda b, p, pt: (b, 0, 0)),
                  pl.BlockSpec((1, wlen, 2 * NSA_KVW), lambda b, p, pt: (b, 0, 0)),
                  pl.BlockSpec(w2.shape, lambda b, p, pt: (0, 0, 0, 0),
                               pipeline_mode=pl.Buffered(1)),
                  pl.BlockSpec(pe.shape, lambda b, p, pt: (0, 0, 0, 0),
                               pipeline_mode=pl.Buffered(1))],
        out_specs=pl.BlockSpec((1, NSA_HEADS, NSA_KVW), lambda b, p, pt: (b, 0, 0)),
        scratch_shapes=[pltpu.VMEM((past, NSA_KVW), F32)] * 4)
    return pl.pallas_call(
        functools.partial(_nsa_sample_kernel, npages=npages, page=page, past=past),
        grid_spec=grid_spec,
        out_shape=jax.ShapeDtypeStruct((nd, NSA_HEADS, NSA_KVW), F32),
        compiler_params=_cparams(("parallel", "arbitrary")),
        name="nsa_sample",
    )(page_table.reshape(-1), cache3, qbd, gts3, kvn3, wkvn3, win3, w2, pe)


_PAGE_ROWS = 4 * NSA_KVW


def _page_copies(cache_ref, pt_ref, buf, sem, seq, npages):
    return [pltpu.make_async_copy(cache_ref.at[pt_ref[seq * npages + p]],
                                  buf.at[pl.ds(p * _PAGE_ROWS, _PAGE_ROWS), :], sem)
            for p in range(npages)]


def _sample_consts(w_cmp, pe_cmp, npages, page):
    bpp = page // BLOCK
    z = jnp.zeros((BLOCK, NSA_HD, NSA_HD), w_cmp.dtype)
    per_c = []
    for c in range(2):
        w = w_cmp[c].transpose(1, 0, 2)
        rows = [jnp.concatenate([w if b2 == b1 else z for b2 in range(bpp)], axis=2)
                for b1 in range(bpp)]
        per_c.append(jnp.concatenate(rows, axis=1))
    w2t = jnp.stack(per_c).astype(BF16)
    pet = jnp.stack([jnp.tile(pe_cmp[c].T, (1, bpp)) for c in range(2)])
    pet = pet.reshape(2, NSA_HD, 1, page)
    j = np.arange(bpp * npages)
    col = np.arange(npages * page)
    expm = (j[:, None] == ((col % page) // BLOCK) * npages + col // page)
    return w2t, pet, jnp.asarray(expm.astype(np.float32), BF16)


def _sample_compute(buf, j, qc_ref, qbd_ref, gts_ref, kvn_ref, wkvn_ref, wcol_ref, win_ref,
                    w2t_ref, pet_ref, exp_ref, o_ref, wout_ref, *, npages, page):
    bpp = page // BLOCK
    nb = npages * bpp
    pos = npages * page
    cur = pos // BLOCK
    nh = NSA_HEADS
    dims_t = (((1,), (1,)), ((), ()))
    lane = lax.broadcasted_iota(jnp.int32, (nh, NSA_KVW), 1)
    qf = _bf16_round(qbd_ref[j] * (NSA_HD ** -0.5))
    qb = qf.astype(BF16)
    kvn = kvn_ref[j]
    wkvn = wkvn_ref[j]

    cmp_ = []
    for c in range(2):
        per_g = []
        for g in range(NSA_KV_HEADS):
            acc = jnp.zeros((npages, page), F32)
            for d in range(NSA_HD):
                r = c * NSA_KVW + g * NSA_HD + d
                x = (buf[pl.ds(r, npages, stride=_PAGE_ROWS), :] + pet_ref[c, d]).astype(BF16)
                acc += jnp.dot(x, w2t_ref[c, d], preferred_element_type=F32)
            per_g.append(acc)
        cmp_.append(jnp.concatenate(per_g, axis=1).astype(BF16))
    ck, cv = cmp_

    pidx = lax.broadcasted_iota(jnp.int32, (nh, npages), 1)
    sc, valid = [], []
    for bb in range(bpp):
        s = lax.dot_general((qc_ref[j, bb] * (NSA_HD ** -0.5)).astype(BF16), ck, dims_t,
                            preferred_element_type=F32)
        v = (bpp * pidx + bb + 1) * BLOCK - 1 <= pos
        sc.append(jnp.where(v, s, NEG))
        valid.append(v)
    mc = functools.reduce(jnp.maximum, [jnp.max(s, axis=1, keepdims=True) for s in sc])
    pc = [jnp.where(v, jnp.exp(s - mc), 0.0) for s, v in zip(sc, valid)]
    lc = functools.reduce(jnp.add, [jnp.sum(p, axis=1, keepdims=True) for p in pc])
    inv_c = 1.0 / jnp.maximum(lc, 1e-30)
    pc = [p * inv_c for p in pc]
    halves = []
    for g in range(NSA_KV_HEADS):
        t = None
        for bb in range(bpp):
            r = jnp.dot(pc[bb].astype(BF16), cv[:, g * NSA_KVW:(g + 1) * NSA_KVW],
                        preferred_element_type=F32)
            r = jnp.where((lane >= bb * BLOCK) & (lane < (bb + 1) * BLOCK), r, 0.0)
            t = r if t is None else t + r
        halves.append(t + pltpu.roll(t, BLOCK, axis=1))
    oc = jnp.where(lane < NSA_HD, halves[0], halves[1])

    width = -(-(nb + 1) // 128) * 128
    imp = [jnp.concatenate([jnp.broadcast_to(
        jnp.sum(p[g * NSA_HPG:(g + 1) * NSA_HPG], axis=0, keepdims=True), (NSA_HPG, npages))
        for g in range(NSA_KV_HEADS)], axis=0) for p in pc]
    imp = jnp.concatenate(imp + [jnp.zeros((nh, width - nb), F32)], axis=1)
    li = lax.broadcasted_iota(jnp.int32, (nh, width), 1)
    assert npages & (npages - 1) == 0
    shift = npages.bit_length() - 1
    bi = jnp.where(li < nb, bpp * jnp.bitwise_and(li, npages - 1)
                   + lax.shift_right_logical(li, shift), li)
    bf = bi.astype(F32)
    causal = bi <= cur
    forced = jnp.logical_and(causal, jnp.logical_or(bi == 0, bi > cur - N_LOCAL))
    score = jnp.where(forced, FORCE_SCORE, jnp.where(causal, imp, -1.0))
    score = jnp.where(li <= nb, score, -3.0)

    def pick_one(_, c):
        score, sel = c
        mx = jnp.max(score, axis=1, keepdims=True)
        idx = jnp.min(jnp.where(score == mx, bf, float(2 * width)), axis=1, keepdims=True)
        pick = bf == idx
        return jnp.where(pick, -4.0, score), jnp.where(pick, 1.0, sel)

    _, sel = lax.fori_loop(0, min(N_SELECT, nb + 1), pick_one,
                           (score, jnp.zeros((nh, width), F32)))
    selb = jnp.where(sel > 0.0, 0.0, NEG)
    bias = jnp.dot(selb[:, :nb].astype(BF16), exp_ref[...], preferred_element_type=F32)
    selb_new = selb[:, nb:nb + 1]

    s_parts = []
    for p in range(npages):
        kt = buf[p * _PAGE_ROWS + 2 * NSA_KVW:p * _PAGE_ROWS + 3 * NSA_KVW, :].astype(BF16)
        s_parts.append(jnp.dot(qb, kt, preferred_element_type=F32)
                       + bias[:, p * page:(p + 1) * page])
    s_new = jnp.sum(qf * _bf16_round(kvn[:, 2 * NSA_KVW:3 * NSA_KVW]), axis=1,
                    keepdims=True) + selb_new
    p_parts, p_new, inv = _softmax_rows(s_parts, s_new)
    acc = _bf16_round(p_new) * _bf16_round(kvn[:, 3 * NSA_KVW:])
    for p in range(npages):
        vt = buf[p * _PAGE_ROWS + 3 * NSA_KVW:(p + 1) * _PAGE_ROWS, :].astype(BF16)
        acc += lax.dot_general(p_parts[p].astype(BF16), vt, dims_t, preferred_element_type=F32)
    os_ = acc * inv

    win = win_ref[j]
    wlen = win.shape[1]
    sw = jnp.dot(qb, win[:NSA_KVW].astype(BF16), preferred_element_type=F32)
    wpos = pos - wlen + lax.broadcasted_iota(jnp.int32, (nh, wlen), 1)
    dist = pos - wpos
    sw = jnp.where(jnp.logical_and(jnp.logical_and(dist >= 0, dist <= WINDOW), wpos >= 0),
                   sw, NEG)
    sw_new = jnp.sum(qf * _bf16_round(wkvn[:, :NSA_KVW]), axis=1, keepdims=True)
    (pw,), pw_new, inv_w = _softmax_rows([sw], sw_new)
    ow = (lax.dot_general(pw.astype(BF16), win[NSA_KVW:].astype(BF16), dims_t,
                          preferred_element_type=F32)
          + _bf16_round(pw_new) * _bf16_round(wkvn[:, NSA_KVW:])) * inv_w

    gt = _sigmoid(gts_ref[j])
    o_ref[j] = gt[:, 0:1] * oc + gt[:, 1:2] * os_ + gt[:, 2:3] * ow

    wl = lax.broadcasted_iota(jnp.int32, win.shape, 1)
    wout_ref[j] = jnp.where(wl == wlen - 1, wcol_ref[j], pltpu.roll(win, wlen - 1, axis=1))


def _nsa_sample_t_kernel(pt_ref, cache_ref, qc_ref, qbd_ref, gts_ref, kvn_ref, wkvn_ref, wcol_ref,
                         win_ref, w2t_ref, pet_ref, exp_ref, o_ref, wout_ref, buf0, buf1, sem,
                         *, npages, page):
    s = pl.program_id(0)
    compute = functools.partial(_sample_compute, qc_ref=qc_ref, qbd_ref=qbd_ref, gts_ref=gts_ref,
                                kvn_ref=kvn_ref, wkvn_ref=wkvn_ref, wcol_ref=wcol_ref,
                                win_ref=win_ref, w2t_ref=w2t_ref, pet_ref=pet_ref, exp_ref=exp_ref,
                                o_ref=o_ref, wout_ref=wout_ref, npages=npages, page=page)
    copies = lambda seq, buf, slot: _page_copies(cache_ref, pt_ref, buf, sem.at[slot], seq, npages)

    @pl.when(s == 0)
    def _():
        for c in copies(0, buf0, 0):
            c.start()

    for c in copies(2 * s + 1, buf1, 1):
        c.start()
    for c in copies(2 * s, buf0, 0):
        c.wait()
    compute(buf0, 0)

    @pl.when(s + 1 < pl.num_programs(0))
    def _():
        for c in copies(2 * s + 2, buf0, 0):
            c.start()

    for c in copies(2 * s + 1, buf1, 1):
        c.wait()
    compute(buf1, 1)


def _nsa_sample_t(page_table, cache_t, qc, qbd, gts3, kvn3, wkvn3, wcol, win_t, w2t, pet, expm):
    nd, npages = page_table.shape
    page = cache_t.shape[2]
    wlen = win_t.shape[2]
    assert nd % 2 == 0 and page % BLOCK == 0
    two = lambda *tail: pl.BlockSpec((2,) + tail, lambda s, pt: (s,) + (0,) * len(tail))
    const = lambda a: pl.BlockSpec(a.shape, lambda s, pt: (0,) * a.ndim,
                                   pipeline_mode=pl.Buffered(1))
    grid_spec = pltpu.PrefetchScalarGridSpec(
        num_scalar_prefetch=1,
        grid=(nd // 2,),
        in_specs=[pl.BlockSpec(memory_space=pl.ANY),
                  two(page // BLOCK, NSA_HEADS, 2 * NSA_KVW), two(NSA_HEADS, NSA_KVW),
                  two(NSA_HEADS, NSA_KVW), two(1, 4 * NSA_KVW), two(1, 2 * NSA_KVW),
                  two(2 * NSA_KVW, 1), two(2 * NSA_KVW, wlen), const(w2t), const(pet), const(expm)],
        out_specs=[two(NSA_HEADS, NSA_KVW), two(2 * NSA_KVW, wlen)],
        scratch_shapes=[pltpu.VMEM((npages * _PAGE_ROWS, page), F32),
                        pltpu.VMEM((npages * _PAGE_ROWS, page), F32),
                        pltpu.SemaphoreType.DMA((2,))])
    return pl.pallas_call(
        functools.partial(_nsa_sample_t_kernel, npages=npages, page=page),
        grid_spec=grid_spec,
        out_shape=[jax.ShapeDtypeStruct((nd, NSA_HEADS, NSA_KVW), F32),
                   jax.ShapeDtypeStruct(win_t.shape, F32)],
        compiler_params=_cparams(("arbitrary",)),
        name="nsa_sample",
    )(page_table.reshape(-1), cache_t, qc, qbd, gts3, kvn3, wkvn3, wcol, win_t, w2t, pet, expm)


def _gla_sample_kernel(gq_ref, gk_ref, gv_ref, misc_ref, w2_ref, ba_ref, s_ref, o_ref, sout_ref,
                       *, ns):
    la = _log_decay(misc_ref[...], w2_ref, ba_ref)
    pad = jnp.zeros((128 - 3 * ns, GLA_KW), F32)
    zt = jnp.concatenate([gq_ref[...] * (GLA_DK ** -0.5), gk_ref[...], jnp.exp(la), pad],
                         axis=0).T
    v = gv_ref[...]
    for i in range(ns):
        outs = []
        for h in range(GLA_HEADS):
            hr = slice(h * GLA_DK, (h + 1) * GLA_DK)
            qc = zt[hr, i:i + 1]
            kc = zt[hr, ns + i:ns + i + 1]
            ac = zt[hr, 2 * ns + i:2 * ns + i + 1]
            s_new = ac * s_ref[i, hr, :] + kc * v[i:i + 1, h * GLA_DV:(h + 1) * GLA_DV]
            sout_ref[i, hr, :] = s_new
            outs.append(jnp.sum(qc * s_new, axis=0, keepdims=True))
        o_ref[i:i + 1, :] = jnp.concatenate(outs, axis=1)


def _gla_sample(gq, gk, gv, misc2d, w2, ba, state3, ns=32):
    nd = gq.shape[0]
    row = lambda w: pl.BlockSpec((ns, w), lambda i: (i, 0))
    st = pl.BlockSpec((ns, GLA_KW, GLA_DV), lambda i: (i, 0, 0))
    return pl.pallas_call(
        functools.partial(_gla_sample_kernel, ns=ns),
        grid=(nd // ns,),
        in_specs=[row(GLA_KW), row(GLA_KW), row(GLA_WIDTH), row(MISC_W),
                  _const_spec(w2.shape), _const_spec(ba.shape), st],
        out_specs=[row(GLA_WIDTH), st],
        out_shape=[jax.ShapeDtypeStruct((nd, GLA_WIDTH), F32),
                   jax.ShapeDtypeStruct(state3.shape, F32)],
        compiler_params=_cparams(("parallel",)),
        name="gla_sample",
    )(gq, gk, gv, misc2d, w2, ba, state3)


def _win_shift_kernel(win_ref, new_ref, o_ref, *, ns):
    wlen = win_ref.shape[1]
    last = lax.broadcasted_iota(jnp.int32, win_ref.shape[1:], 0) == wlen - 1
    for i in range(ns):
        o_ref[i] = jnp.where(last, new_ref[i], pltpu.roll(win_ref[i], wlen - 1, axis=0))


def _win_shift(win3, new3, ns=8):
    nd, wlen, w = win3.shape
    return pl.pallas_call(
        functools.partial(_win_shift_kernel, ns=ns),
        grid=(nd // ns,),
        in_specs=[pl.BlockSpec((ns, wlen, w), lambda i: (i, 0, 0)),
                  pl.BlockSpec((ns, 1, w), lambda i: (i, 0, 0))],
        out_specs=pl.BlockSpec((ns, wlen, w), lambda i: (i, 0, 0)),
        out_shape=jax.ShapeDtypeStruct(win3.shape, F32),
        compiler_params=_cparams(("parallel",)),
        name="win_shift",
    )(win3, new3)


def kernel(x_prompt, x_sample, cache_kv, cache_win, state_gla, page_table, p_prompt, p_sample,
           g_attn, w_in, w_cmp, pe_cmp, g_nsa_out, w_gla_a2, b_gla_a, g_gla_out, w_out, g_ffn,
           w_gate_up, w_down, g_ple, w_ple_gate, w_ple_proj, g_final):
    assert g_attn.shape[0] == 1, "single-layer trunk"
    n, s, d = x_prompt.shape
    nd, ds, _ = x_sample.shape
    assert ds == 1 and s % Q_BLOCK == 0 and s >= WINDOW
    w1 = _reorder_w_in(w_in[0])
    fw = _ffn_weights(g_nsa_out[0], g_gla_out[0], w_out[0], g_ffn[0], w_gate_up[0], w_down[0],
                      g_ple[0], w_ple_gate[0], w_ple_proj[0], g_final)
    cw2, cpe = _cmp_weights(w_cmp[0], pe_cmp[0])
    gw2, gba = _gla_gate_weights(w_gla_a2[0], b_gla_a[0])

    xp = x_prompt.reshape(n * s, d)
    q, kv, wkv, gq, gk, gv, gr, misc, kvt, wkvt = _project(xp, g_attn[0], w1, n, min(512, s))
    cmat, ksl, vst, kw, vwt = _nsa_prep(kv, wkv, kvt, wkvt, cw2, cpe, n, s, min(4096, s))
    o_n = _nsa_prompt(q, misc, cmat, ksl, vst, kw, vwt, n, s)
    o_g, s_p = _gla_prompt(gq, gk, gv, misc, gw2, gba, n, s, min(512, s))
    y_p = _mixer_ffn(o_n, o_g, gr, xp, p_prompt[0].reshape(n * s, -1), fw, min(256, n * s))

    xs = x_sample.reshape(nd, d)
    qs, kvs, wkvs, gqs, gks, gvs, grs, miscs, kvts, _ = _project(xs, g_attn[0], w1, 1, nd)
    npool, page = cache_kv.shape[1], cache_kv.shape[2]
    npages = page_table.shape[1]
    cache_t = _to_channel_major(cache_kv[0])
    win_t = _to_channel_major(cache_win[0])
    q4 = qs.reshape(nd, NSA_KV_HEADS, NSA_HPG, NSA_HD)
    qbd = jnp.concatenate([_place(q4[:, g], g, NSA_KV_HEADS) for g in range(NSA_KV_HEADS)],
                          axis=1)
    bpp = page // BLOCK
    qc = jnp.stack([jnp.concatenate([_place(q4[:, g], g * bpp + bb, NSA_KV_HEADS * bpp)
                                     for g in range(NSA_KV_HEADS)], axis=1)
                    for bb in range(bpp)], axis=1)
    gts3 = jnp.pad(miscs[:, :N_GATES].reshape(nd, NSA_HEADS, 3), ((0, 0), (0, 0), (0, NSA_KVW - 3)))
    w2t, pet, expm = _sample_consts(w_cmp[0], pe_cmp[0], npages, page)
    o8, win_new = _nsa_sample_t(page_table, cache_t, qc, qbd, gts3, kvs.reshape(nd, 1, -1),
                                wkvs.reshape(nd, 1, -1), wkvs.reshape(nd, -1, 1), win_t,
                                w2t, pet, expm)
    o8 = o8.reshape(nd, NSA_KV_HEADS, NSA_HPG, NSA_KV_HEADS, NSA_HD)
    o_ns = jnp.stack([o8[:, 0, :, 0], o8[:, 1, :, 1]], axis=1).reshape(nd, NSA_WIDTH)
    o_gs, s_s = _gla_sample(gqs, gks, gvs, miscs, gw2, gba,
                            state_gla[0].reshape(nd, GLA_KW, GLA_DV))
    y_s = _mixer_ffn(o_ns, o_gs, grs, xs, p_sample[0].reshape(nd, -1), fw, nd)

    wkeep = min(WINDOW, s)
    return (y_p.reshape(n, s, d),
            y_s.reshape(nd, 1, d),
            _from_channel_major(kvt, 4)[None],
            _from_channel_major(wkvt[:, :, s - wkeep:], 2)[None],
            s_p.reshape(1, n, GLA_HEADS, GLA_DK, GLA_DV),
            _from_channel_major(kvts, 4).reshape(1, nd, 1, 4, NSA_KV_HEADS, NSA_HD),
            _from_channel_major(win_new, 2)[None],
            s_s.reshape(1, nd, GLA_HEADS, GLA_DK, GLA_DV))


def _to_channel_major(x):
    n, t = x.shape[:2]
    return jnp.transpose(x, (0, 2, 3, 4, 1)).reshape(n, -1, t)


def _from_channel_major(xt, c):
    n, _, t = xt.shape
    return jnp.transpose(xt.reshape(n, c, NSA_KV_HEADS, NSA_HD, t), (0, 4, 1, 2, 3))


def _place(x, slot, nslots):
    z = jnp.zeros_like(x)
    return jnp.concatenate([x if i == slot else z for i in range(nslots)], axis=-1)
```

```python
import functools

import numpy as np
import jax
import jax.numpy as jnp
from jax import lax
from jax.experimental import pallas as pl
from jax.experimental.pallas import tpu as pltpu

F32 = jnp.float32
BF16 = jnp.bfloat16

EPS = 1e-6
NEG = -1e30
LOG2E = 1.4426950408889634

NSA_HEADS = 8
NSA_KV_HEADS = 2
NSA_HPG = NSA_HEADS // NSA_KV_HEADS
NSA_HD = 64
NSA_WIDTH = NSA_HEADS * NSA_HD
NSA_KVW = NSA_KV_HEADS * NSA_HD
BLOCK = 64
N_SELECT = 16
N_LOCAL = 2
WINDOW = 512
Q_BLOCK = 128
FORCE_SCORE = float(NSA_HPG + 1)
GLA_HEADS = 4
GLA_DK = 64
GLA_DV = 128
GLA_WIDTH = GLA_HEADS * GLA_DV
GLA_KW = GLA_HEADS * GLA_DK
GLA_GATE_RANK = 16
GLA_GATE_TEMP = 16.0
GLA_CHUNK = 64
N_GATES = 3 * NSA_HEADS
MISC_W = 128

VMEM_LIMIT = 56 * 1024 * 1024


def _cparams(sem):
    return pltpu.CompilerParams(dimension_semantics=sem, vmem_limit_bytes=VMEM_LIMIT)


def _const_spec(shape):
    nd = len(shape)
    return pl.BlockSpec(shape, lambda *_: (0,) * nd, pipeline_mode=pl.Buffered(1))


def _sigmoid(x):
    return 1.0 / (1.0 + jnp.exp(-x))


def _rms(x, g):
    ms = jnp.mean(x * x, axis=-1, keepdims=True)
    return x * lax.rsqrt(ms + EPS) * g


def _split3(x):
    hi = x.astype(BF16)
    r = x - hi.astype(F32)
    mid = r.astype(BF16)
    lo = (r - mid.astype(F32)).astype(BF16)
    return hi, mid, lo


_P_Q, _P_KV, _P_WKV, _P_GQ, _P_GK, _P_GV, _P_GR, _P_MISC, _P_END = (
    0, 512, 1024, 1280, 1536, 1792, 2304, 2816, 2944)


def _reorder_w_in(w_in):
    offs = np.cumsum([0, NSA_WIDTH, 4 * NSA_KVW, 2 * NSA_KVW, N_GATES, GLA_KW, GLA_KW,
                      GLA_WIDTH, GLA_GATE_RANK, GLA_WIDTH])
    q, kv, wkv, gts, gq, gk, gv, ga, gr = [w_in[:, offs[i]:offs[i + 1]] for i in range(9)]
    pad = jnp.zeros((w_in.shape[0], MISC_W - N_GATES - GLA_GATE_RANK), w_in.dtype)
    return jnp.concatenate([q, kv, wkv, gq, gk, gv, gr, gts, ga, pad], axis=1).astype(BF16)


def _proj_kernel(x_ref, g_ref, w_ref, wt_ref, q_ref, kv_ref, wkv_ref, gq_ref, gk_ref, gv_ref,
                 gr_ref, misc_ref, kvt_ref, wkvt_ref):
    xn = _rms(x_ref[...], g_ref[...]).astype(BF16)
    for ref, a, b in ((q_ref, _P_Q, _P_KV), (kv_ref, _P_KV, _P_WKV), (wkv_ref, _P_WKV, _P_GQ),
                      (gq_ref, _P_GQ, _P_GK), (gk_ref, _P_GK, _P_GV), (gv_ref, _P_GV, _P_GR),
                      (gr_ref, _P_GR, _P_MISC), (misc_ref, _P_MISC, _P_END)):
        ref[...] = jnp.dot(xn, w_ref[:, a:b], preferred_element_type=F32)
    dims_t = (((1,), (1,)), ((), ()))
    nkv = kvt_ref.shape[1]
    kvt_ref[0] = lax.dot_general(wt_ref[0:nkv, :], xn, dims_t, preferred_element_type=F32)
    wkvt_ref[0] = lax.dot_general(wt_ref[nkv:, :], xn, dims_t, preferred_element_type=F32)


def _project(x2d, g_attn, w1, n, tm):
    t, d = x2d.shape
    s = t // n
    nj = s // tm
    widths = (512, 512, 256, 256, 256, 512, 512, MISC_W)
    wt = w1[:, _P_KV:_P_GQ].T
    tspec = lambda rows: pl.BlockSpec((1, rows, tm), lambda i: (i // nj, 0, i % nj))
    return pl.pallas_call(
        _proj_kernel,
        grid=(t // tm,),
        in_specs=[pl.BlockSpec((tm, d), lambda i: (i, 0)),
                  _const_spec((1, d)),
                  _const_spec(w1.shape),
                  _const_spec(wt.shape)],
        out_specs=[pl.BlockSpec((tm, w), lambda i: (i, 0)) for w in widths]
                  + [tspec(4 * NSA_KVW), tspec(2 * NSA_KVW)],
        out_shape=[jax.ShapeDtypeStruct((t, w), F32) for w in widths]
                  + [jax.ShapeDtypeStruct((n, 4 * NSA_KVW, s), F32),
                     jax.ShapeDtypeStruct((n, 2 * NSA_KVW, s), F32)],
        compiler_params=_cparams(("parallel",)),
        name="proj",
    )(x2d, g_attn.reshape(1, d), w1, wt)


def _group_mean_sq(x, bd_ref, width):
    hi, mid, lo = _split3(x * x)
    bd = bd_ref[...]
    s = (jnp.dot(hi, bd, preferred_element_type=F32) + jnp.dot(mid, bd, preferred_element_type=F32)
         + jnp.dot(lo, bd, preferred_element_type=F32))
    return s * (1.0 / width)


def _ffn_kernel(on_ref, og_ref, gr_ref, x_ref, p_ref, gn_ref, gg_ref, bd64_ref, bd128_ref,
                wo_ref, gffn_ref, wgu_ref, wd_ref, gple_ref, wpg_ref, wpp_ref, gfin_ref,
                y_ref, h_ref, acc_ref, *, n_ff):
    o_n = on_ref[...]
    o_g = og_ref[...]
    r = gr_ref[...]
    a_n = o_n * lax.rsqrt(_group_mean_sq(o_n, bd64_ref, NSA_HD) + EPS) * gn_ref[...]
    a_g = o_g * lax.rsqrt(_group_mean_sq(o_g, bd128_ref, GLA_DV) + EPS) * gg_ref[...]
    a_g = a_g * (r * _sigmoid(r))
    x1 = (x_ref[...]
          + jnp.dot(a_n.astype(BF16), wo_ref[0:NSA_WIDTH, :], preferred_element_type=F32)
          + jnp.dot(a_g.astype(BF16), wo_ref[NSA_WIDTH:, :], preferred_element_type=F32))
    h_ref[...] = _rms(x1, gffn_ref[...]).astype(BF16)
    acc_ref[...] = x1

    def body(c, carry):
        gu = jnp.dot(h_ref[...], wgu_ref[c], preferred_element_type=F32)
        half = gu.shape[1] // 2
        gt = gu[:, :half]
        act = (gt * _sigmoid(gt) * gu[:, half:]).astype(BF16)
        acc_ref[...] += jnp.dot(act, wd_ref[c], preferred_element_type=F32)
        return carry

    lax.fori_loop(0, n_ff, body, 0)
    x2 = acc_ref[...]
    gate = _sigmoid(jnp.dot(_rms(x2, gple_ref[...]).astype(BF16), wpg_ref[...],
                            preferred_element_type=F32))
    x3 = x2 + jnp.dot(p_ref[...].astype(BF16), wpp_ref[...], preferred_element_type=F32) * gate
    y_ref[...] = _rms(x3, gfin_ref[...])


def _block_diag_ones(n, width):
    idx = np.arange(n) // width
    return jnp.asarray((idx[:, None] == idx[None, :]).astype(np.float32), BF16)


def _ffn_weights(g_nsa_out, g_gla_out, w_out, g_ffn, w_gate_up, w_down, g_ple, w_ple_gate,
                 w_ple_proj, g_final, ff_chunk=256):
    d = w_out.shape[1]
    d_ff = w_down.shape[0]
    n_ff = d_ff // ff_chunk
    wg = w_gate_up[:, :d_ff].reshape(d, n_ff, ff_chunk)
    wu = w_gate_up[:, d_ff:].reshape(d, n_ff, ff_chunk)
    wgu = jnp.concatenate([wg, wu], axis=2).transpose(1, 0, 2).astype(BF16)
    wd = w_down.reshape(n_ff, ff_chunk, d).astype(BF16)
    return dict(gn=g_nsa_out.reshape(1, NSA_WIDTH), gg=g_gla_out.reshape(1, GLA_WIDTH),
                bd64=_block_diag_ones(NSA_WIDTH, NSA_HD), bd128=_block_diag_ones(GLA_WIDTH, GLA_DV),
                wo=w_out.astype(BF16), gffn=g_ffn.reshape(1, d), wgu=wgu, wd=wd,
                gple=g_ple.reshape(1, d), wpg=w_ple_gate.astype(BF16),
                wpp=w_ple_proj.astype(BF16), gfin=g_final.reshape(1, d))


def _mixer_ffn(o_n, o_g, gr, x2d, p2d, fw, tm):
    t, d = x2d.shape
    n_ff = fw["wgu"].shape[0]
    consts = [fw[k] for k in ("gn", "gg", "bd64", "bd128", "wo", "gffn", "wgu", "wd", "gple",
                              "wpg", "wpp", "gfin")]
    row = lambda w: pl.BlockSpec((tm, w), lambda i: (i, 0))
    return pl.pallas_call(
        functools.partial(_ffn_kernel, n_ff=n_ff),
        grid=(t // tm,),
        in_specs=[row(NSA_WIDTH), row(GLA_WIDTH), row(GLA_WIDTH), row(d), row(p2d.shape[1])]
                 + [_const_spec(c.shape) for c in consts],
        out_specs=row(d),
        out_shape=jax.ShapeDtypeStruct((t, d), F32),
        scratch_shapes=[pltpu.VMEM((tm, d), BF16), pltpu.VMEM((tm, d), F32)],
        compiler_params=_cparams(("parallel",)),
        name="mixer_ffn",
    )(o_n, o_g, gr, x2d, p2d, *consts)


def _cmp_weights(w_cmp, pe_cmp):
    z = jnp.zeros((BLOCK, NSA_HD, NSA_HD), w_cmp.dtype)

    def bd(w):
        return jnp.concatenate([jnp.concatenate([w, z], axis=2),
                                jnp.concatenate([z, w], axis=2)], axis=1)

    w2 = jnp.stack([bd(w_cmp[0]), bd(w_cmp[1])], axis=1).astype(BF16)
    pe = jnp.stack([jnp.concatenate([pe_cmp[0], pe_cmp[0]], axis=1),
                    jnp.concatenate([pe_cmp[1], pe_cmp[1]], axis=1)], axis=1)
    return w2, pe.reshape(BLOCK, 2, 1, NSA_KVW)


def _compress(xk_ref, xv_ref, w2_ref, pe_ref, nb):
    acc_k = jnp.zeros((nb, NSA_KVW), F32)
    acc_v = jnp.zeros((nb, NSA_KVW), F32)
    for l in range(BLOCK):
        xk = (xk_ref[pl.ds(l, nb, stride=BLOCK), :] + pe_ref[l, 0]).astype(BF16)
        xv = (xv_ref[pl.ds(l, nb, stride=BLOCK), :] + pe_ref[l, 1]).astype(BF16)
        acc_k += jnp.dot(xk, w2_ref[l, 0], preferred_element_type=F32)
        acc_v += jnp.dot(xv, w2_ref[l, 1], preferred_element_type=F32)
    return acc_k, acc_v


def _prep_kernel(ckin_ref, cvin_ref, ksin_ref, kwin_ref, vst_in_ref, vwt_in_ref, w2_ref, pe_ref,
                 c_ref, ks_ref, vst_ref, kw_ref, vwt_ref, *, tk):
    nb = tk // BLOCK
    ck, cv = _compress(ckin_ref, cvin_ref, w2_ref, pe_ref, nb)
    c_ref[:, 0:NSA_KVW] = ck
    c_ref[:, NSA_KVW:] = cv
    ks_ref[...] = ksin_ref[...].astype(BF16)
    kw_ref[...] = kwin_ref[...].astype(BF16)
    vst_ref[...] = vst_in_ref[...].astype(BF16)
    vwt_ref[...] = vwt_in_ref[...].astype(BF16)


def _nsa_prep(kv2d, wkv2d, kvt, wkvt, w2, pe, n, s, tk):
    t = n * s
    nj = s // tk
    kv_specs = [pl.BlockSpec((tk, NSA_KVW), (lambda b, j, c=c: (b * nj + j, c))) for c in range(3)]
    return pl.pallas_call(
        functools.partial(_prep_kernel, tk=tk),
        grid=(n, nj),
        in_specs=kv_specs + [pl.BlockSpec((tk, NSA_KVW), lambda b, j: (b * nj + j, 0)),
                             pl.BlockSpec((1, NSA_KVW, tk), lambda b, j: (b, 3, j)),
                             pl.BlockSpec((1, NSA_KVW, tk), lambda b, j: (b, 1, j)),
                             _const_spec(w2.shape), _const_spec(pe.shape)],
        out_specs=[pl.BlockSpec((tk // BLOCK, 2 * NSA_KVW), lambda b, j: (b * nj + j, 0)),
                   pl.BlockSpec((tk, NSA_KVW), lambda b, j: (b * nj + j, 0)),
                   pl.BlockSpec((1, NSA_KVW, tk), lambda b, j: (b, 0, j)),
                   pl.BlockSpec((tk, NSA_KVW), lambda b, j: (b * nj + j, 0)),
                   pl.BlockSpec((1, NSA_KVW, tk), lambda b, j: (b, 0, j))],
        out_shape=[jax.ShapeDtypeStruct((t // BLOCK, 2 * NSA_KVW), F32),
                   jax.ShapeDtypeStruct((t, NSA_KVW), BF16),
                   jax.ShapeDtypeStruct((n, NSA_KVW, s), BF16),
                   jax.ShapeDtypeStruct((t, NSA_KVW), BF16),
                   jax.ShapeDtypeStruct((n, NSA_KVW, s), BF16)],
        compiler_params=_cparams(("parallel", "parallel")),
        name="nsa_prep",
    )(kv2d, kv2d, kv2d, wkv2d, kvt, wkvt, w2, pe)


_LQ = NSA_KV_HEADS * NSA_HPG * Q_BLOCK
_LG = NSA_HPG * Q_BLOCK
_SEL_GROUP = 4 * Q_BLOCK


def _rep_heads(x):
    a, b = x[:, :Q_BLOCK], x[:, Q_BLOCK:]
    return jnp.concatenate([a] * NSA_HPG + [b] * NSA_HPG, axis=1)


def _flash_tile(k, vt, qt, carry, mask=None):
    s = jnp.dot(k, qt, preferred_element_type=F32)
    if mask is not None:
        s = jnp.where(mask, s, NEG)
    return _flash_update(s, vt, carry)


def _flash_update(s, vt, carry):
    m, l, acc0, acc1 = carry
    m_new = jnp.maximum(m, jnp.max(s, axis=0, keepdims=True))
    alpha = jnp.exp2(m - m_new)
    p = jnp.exp2(s - m_new)
    l = alpha * l + jnp.sum(p, axis=0, keepdims=True)
    pb = p.astype(BF16)
    acc0 = alpha[:, :_LG] * acc0 + jnp.dot(vt[:NSA_HD], pb[:, :_LG], preferred_element_type=F32)
    acc1 = alpha[:, _LG:] * acc1 + jnp.dot(vt[NSA_HD:], pb[:, _LG:], preferred_element_type=F32)
    return m_new, l, acc0, acc1


def _flash_init():
    return (jnp.full((1, _LQ), NEG, F32), jnp.zeros((1, _LQ), F32),
            jnp.zeros((NSA_HD, _LG), F32), jnp.zeros((NSA_HD, _LG), F32))


def _flash_out(carry):
    m, l, acc0, acc1 = carry
    inv = 1.0 / jnp.maximum(l, 1e-30)
    return acc0 * inv[:, :_LG], acc1 * inv[:, _LG:]


def _select_blocks(imp, cur, nbs):
    if nbs <= N_SELECT:
        return jnp.ones(imp.shape, F32)
    bi = lax.broadcasted_iota(jnp.int32, imp.shape, 0)
    bf = bi.astype(F32)
    causal = bi <= cur
    forced = jnp.logical_and(causal, jnp.logical_or(bi == 0, bi > cur - N_LOCAL))
    score = jnp.where(forced, -2.0, jnp.where(causal, imp, -1.0))

    def body(_, c):
        score, sel = c
        mx = jnp.max(score, axis=0, keepdims=True)
        idx = jnp.min(jnp.where(score == mx, bf, float(nbs)), axis=0, keepdims=True)
        pick = bf == idx
        return jnp.where(pick, -2.0, score), jnp.where(pick, 1.0, sel)

    _, sel = lax.fori_loop(0, N_SELECT - 1 - N_LOCAL, body,
                           (score, jnp.where(forced, 1.0, 0.0)))
    return sel


def _nsa_prompt_kernel(q_ref, misc_ref, c_ref, ks_ref, vst_ref, ind_ref, *rest, nbc):
    kw_refs = rest[0:5]
    vwt_refs = rest[5:10]
    o_ref = rest[10]
    selb_ref, s0_ref, s1_ref, p0_ref, p1_ref = rest[11:16]
    i = pl.program_id(1)

    qT = (q_ref[...] * (NSA_HD ** -0.5 * LOG2E)).T
    z = jnp.zeros((NSA_HD, Q_BLOCK), F32)
    top = [qT[NSA_HD * h:NSA_HD * (h + 1)] for h in range(NSA_HPG)] + [z] * NSA_HPG
    bot = [z] * NSA_HPG + [qT[NSA_HD * h:NSA_HD * (h + 1)] for h in range(NSA_HPG, NSA_HEADS)]
    qt = jnp.concatenate([jnp.concatenate(top, axis=1), jnp.concatenate(bot, axis=1)],
                         axis=0).astype(BF16)

    cmat = c_ref[...]
    ck = cmat[:, :NSA_KVW].astype(BF16)
    cv = cmat[:, NSA_KVW:].astype(BF16)
    sc = jnp.dot(ck, qt, preferred_element_type=F32)
    b_io = lax.broadcasted_iota(jnp.int32, (nbc, _LQ), 0)
    qoff = jnp.bitwise_and(lax.broadcasted_iota(jnp.int32, (nbc, _LQ), 1), Q_BLOCK - 1)
    valid = b_io * BLOCK + (BLOCK - 1) <= i * Q_BLOCK + qoff
    sc = jnp.where(valid, sc, NEG)
    mc = jnp.max(sc, axis=0, keepdims=True)
    pc = jnp.where(valid, jnp.exp2(sc - mc), 0.0)
    pc = pc / jnp.maximum(jnp.sum(pc, axis=0, keepdims=True), 1e-30)
    oc_full = lax.dot_general(cv, pc.astype(BF16), (((0,), (0,)), ((), ())),
                              preferred_element_type=F32)
    oc = (oc_full[:NSA_HD, :_LG], oc_full[NSA_HD:, _LG:])

    qo = lax.broadcasted_iota(jnp.int32, (1, Q_BLOCK), 1)
    cur = 2 * i + (qo >= BLOCK).astype(jnp.int32)
    for g in range(NSA_KV_HEADS):
        imp = pc[:, g * _LG:g * _LG + Q_BLOCK]
        for hh in range(1, NSA_HPG):
            imp = imp + pc[:, g * _LG + hh * Q_BLOCK:g * _LG + (hh + 1) * Q_BLOCK]
        sel = _select_blocks(imp, cur, nbc)
        selb_ref[:, g * Q_BLOCK:(g + 1) * Q_BLOCK] = jnp.where(sel > 0.0, 0.0, NEG)

    nbg = _SEL_GROUP // BLOCK
    ind = ind_ref[...]
    zpad = jnp.zeros((NSA_KVW - 16, _LQ), BF16)

    def scores(jg):
        off = pl.multiple_of(jg * _SEL_GROUP, _SEL_GROUP)
        ka = jnp.concatenate([ks_ref[pl.ds(off, _SEL_GROUP), :], ind], axis=1)
        brow = selb_ref[pl.ds(pl.multiple_of(jg * nbg, nbg), nbg), :]
        baug = jnp.concatenate([_rep_heads(brow), jnp.zeros((16 - nbg, _LQ), F32)],
                               axis=0).astype(BF16)
        return jnp.dot(ka, jnp.concatenate([qt, baug, zpad], axis=0), preferred_element_type=F32)

    def values(jg, p_ref):
        vt = vst_ref[0, :, pl.ds(pl.multiple_of(jg * _SEL_GROUP, _SEL_GROUP), _SEL_GROUP)]
        return (jnp.dot(vt[:NSA_HD], p_ref[:, :_LG], preferred_element_type=F32),
                jnp.dot(vt[NSA_HD:], p_ref[:, _LG:], preferred_element_type=F32))

    def stage(jg, s_cur, p_cur, s_nxt, p_prev, st, diagonal=False):
        m, l, b0, b1 = st
        if s_nxt is not None:
            s_nxt[...] = scores(jg + 1)
        pv0, pv1 = values(jnp.maximum(jg - 1, 0), p_prev)
        s = s_cur[...]
        if diagonal:
            rowg = lax.broadcasted_iota(jnp.int32, (_SEL_GROUP, _LQ), 0)
            qcg = jnp.bitwise_and(lax.broadcasted_iota(jnp.int32, (_SEL_GROUP, _LQ), 1),
                                  Q_BLOCK - 1)
            s = jnp.where(jg * _SEL_GROUP + rowg <= i * Q_BLOCK + qcg, s, NEG)
        m_new = jnp.maximum(m, jnp.max(s, axis=0, keepdims=True))
        alpha = jnp.exp2(m - m_new)
        p = jnp.exp2(s - m_new)
        p_cur[...] = p.astype(BF16)
        l = alpha * l + jnp.sum(p, axis=0, keepdims=True)
        return m_new, l, alpha[:, :_LG] * (b0 + pv0), alpha[:, _LG:] * (b1 + pv1)

    def finish(st, p_last, jg):
        m, l, b0, b1 = st
        pv0, pv1 = values(jg, p_last)
        inv = 1.0 / jnp.maximum(l, 1e-30)
        return (b0 + pv0) * inv[:, :_LG], (b1 + pv1) * inv[:, _LG:]

    nfull = lax.shift_right_logical(i, 2)
    s0_ref[...] = scores(0)
    p1_ref[...] = jnp.zeros(p1_ref.shape, BF16)

    def pair(t, st):
        st = stage(2 * t, s0_ref, p0_ref, s1_ref, p1_ref, st)
        return stage(2 * t + 1, s1_ref, p1_ref, s0_ref, p0_ref, st)

    st = lax.fori_loop(0, lax.shift_right_logical(nfull, 1), pair, _flash_init())

    def odd_tail(st):
        st = stage(nfull - 1, s0_ref, p0_ref, s1_ref, p1_ref, st)
        st = stage(nfull, s1_ref, p1_ref, None, p0_ref, st, diagonal=True)
        return finish(st, p1_ref, nfull)

    def even_tail(st):
        st = stage(nfull, s0_ref, p0_ref, None, p1_ref, st, diagonal=True)
        return finish(st, p0_ref, nfull)

    os_ = lax.cond(jnp.bitwise_and(nfull, 1) == 1, odd_tail, even_tail, st)

    row = lax.broadcasted_iota(jnp.int32, (Q_BLOCK, _LQ), 0)
    qcol = jnp.bitwise_and(lax.broadcasted_iota(jnp.int32, (Q_BLOCK, _LQ), 1), Q_BLOCK - 1)
    carry = _flash_tile(kw_refs[4][...], vwt_refs[4][0], qt, _flash_init(), mask=row <= qcol)
    for t in range(4):
        first_ok = jnp.where(i - 4 + t >= 0, 0, Q_BLOCK)
        mask = row >= (qcol + first_ok if t == 0 else first_ok)
        carry = _flash_tile(kw_refs[t][...], vwt_refs[t][0], qt, carry, mask=mask)
    ow = _flash_out(carry)

    gt = _sigmoid(misc_ref[...]).T
    outs = []
    for g in range(NSA_KV_HEADS):
        for hh in range(NSA_HPG):
            h = g * NSA_HPG + hh
            sl = slice(hh * Q_BLOCK, (hh + 1) * Q_BLOCK)
            outs.append(gt[3 * h:3 * h + 1] * oc[g][:, sl] + gt[3 * h + 1:3 * h + 2] * os_[g][:, sl]
                        + gt[3 * h + 2:3 * h + 3] * ow[g][:, sl])
    o_ref[...] = jnp.concatenate(outs, axis=0).T


def _nsa_prompt(q2d, misc2d, cmat, ks, vst, kw, vwt, n, s):
    assert s % _SEL_GROUP == 0
    nq = s // Q_BLOCK
    nbc = s // BLOCK
    blk = np.arange(_SEL_GROUP) // BLOCK
    ind = jnp.asarray((blk[:, None] == np.arange(NSA_KVW)[None, :]).astype(np.float32), BF16)
    wk_specs = [pl.BlockSpec((Q_BLOCK, NSA_KVW),
                             (lambda b, i, t=t: (b * nq + jnp.maximum(i - 4 + t, 0), 0)))
                for t in range(5)]
    wv_specs = [pl.BlockSpec((1, NSA_KVW, Q_BLOCK),
                             (lambda b, i, t=t: (b, 0, jnp.maximum(i - 4 + t, 0))))
                for t in range(5)]
    return pl.pallas_call(
        functools.partial(_nsa_prompt_kernel, nbc=nbc),
        grid=(n, nq),
        in_specs=[pl.BlockSpec((Q_BLOCK, NSA_WIDTH), lambda b, i: (b * nq + i, 0)),
                  pl.BlockSpec((Q_BLOCK, MISC_W), lambda b, i: (b * nq + i, 0)),
                  pl.BlockSpec((nbc, 2 * NSA_KVW), lambda b, i: (b, 0)),
                  pl.BlockSpec((s, NSA_KVW), lambda b, i: (b, 0)),
                  pl.BlockSpec((1, NSA_KVW, s), lambda b, i: (b, 0, 0)),
                  _const_spec(ind.shape)] + wk_specs + wv_specs,
        out_specs=pl.BlockSpec((Q_BLOCK, NSA_WIDTH), lambda b, i: (b * nq + i, 0)),
        out_shape=jax.ShapeDtypeStruct((n * s, NSA_WIDTH), F32),
        scratch_shapes=[pltpu.VMEM((nbc, 2 * Q_BLOCK), F32),
                        pltpu.VMEM((_SEL_GROUP, _LQ), F32), pltpu.VMEM((_SEL_GROUP, _LQ), F32),
                        pltpu.VMEM((_SEL_GROUP, _LQ), BF16), pltpu.VMEM((_SEL_GROUP, _LQ), BF16)],
        compiler_params=_cparams(("parallel", "arbitrary")),
        name="nsa_prompt",
    )(q2d, misc2d, cmat, ks, vst, ind, *([kw] * 5), *([vwt] * 5))


_N_LEVELS = int(np.log2(GLA_CHUNK))


def _gla_consts():
    c = GLA_CHUNK
    t = np.arange(c)
    mats = [t[None, :] <= t[:, None], t[None, :] > t[:, None]]
    masks = []
    w = c // 2
    while w >= 1:
        blk = t // (2 * w)
        mid = blk * 2 * w + w
        upper = t >= mid
        m = np.zeros((c, c), bool)
        for r in range(c):
            if upper[r]:
                m[r, mid[r]:r + 1] = True
            else:
                m[r, r + 1:mid[r]] = True
        mats.append(m)
        masks.append((blk[:, None] == blk[None, :]) & upper[:, None] & ~upper[None, :])
        w //= 2
    masks.append(np.eye(c, dtype=bool))
    mall = jnp.asarray(np.concatenate(mats, axis=0).astype(np.float32), BF16)
    lmask = jnp.asarray(np.tile(np.stack(masks).astype(np.float32), (1, GLA_HEADS, 1)))
    hd = np.arange(GLA_KW) // GLA_DK
    hmask = jnp.asarray((hd[:, None] == hd[None, :]).astype(np.float32))
    return mall, lmask, hmask


def _gla_gate_weights(w_a2, b_a):
    w2 = jnp.zeros((MISC_W, GLA_KW), F32).at[N_GATES:N_GATES + GLA_GATE_RANK].set(w_a2)
    return w2.astype(BF16), b_a.reshape(1, GLA_KW)


def _log_decay(misc, w2_ref, ba_ref):
    x = jnp.dot(misc.astype(BF16), w2_ref[...], preferred_element_type=F32) + ba_ref[...]
    return (jnp.minimum(x, 0.0) - jnp.log1p(jnp.exp(-jnp.abs(x)))) * (1.0 / GLA_GATE_TEMP)


def _gla_kernel(gq_ref, gk_ref, gv_ref, misc_ref, w2_ref, ba_ref, mall_ref, lmask_ref, hm_ref,
                o_ref, sout_ref, s_ref, *, nchunk):
    j = pl.program_id(1)

    @pl.when(j == 0)
    def _():
        s_ref[...] = jnp.zeros(s_ref.shape, F32)

    la_all = _log_decay(misc_ref[...], w2_ref, ba_ref)
    mall = mall_ref[...]
    hm = hm_ref[...]
    c = GLA_CHUNK
    dims_t = (((1,), (1,)), ((), ()))

    def stack(x):
        return (jnp.concatenate([x] * GLA_HEADS, axis=0) * hm).astype(BF16)

    for ci in range(nchunk):
        rows = slice(ci * c, (ci + 1) * c)
        q = gq_ref[rows, :] * (GLA_DK ** -0.5)
        k = gk_ref[rows, :]
        v = gv_ref[rows, :].astype(BF16)
        la = la_all[rows]
        hi, mid, lo = _split3(la)
        ex = jnp.exp(jnp.dot(mall, hi, preferred_element_type=F32)
                     + jnp.dot(mall, mid, preferred_element_type=F32)
                     + jnp.dot(mall, lo, preferred_element_type=F32))
        att = lax.dot_general(stack(q), k.astype(BF16), dims_t,
                              preferred_element_type=F32) * lmask_ref[_N_LEVELS]
        for lev in range(_N_LEVELS):
            e = ex[(2 + lev) * c:(3 + lev) * c]
            att += lax.dot_general(stack(q * e), (k * e).astype(BF16), dims_t,
                                   preferred_element_type=F32) * lmask_ref[lev]
        attb = att.astype(BF16)
        s_old = s_ref[...]
        o_inter = jnp.dot(stack(q * ex[0:c]), s_old.astype(BF16), preferred_element_type=F32)
        kd = (k * ex[c:2 * c]).astype(BF16)
        upd = lax.dot_general(kd, v, (((0,), (0,)), ((), ())), preferred_element_type=F32)
        dec = jnp.exp(jnp.sum(la.T, axis=1, keepdims=True))
        outs, news = [], []
        for h in range(GLA_HEADS):
            hr = slice(h * GLA_DK, (h + 1) * GLA_DK)
            hv = slice(h * GLA_DV, (h + 1) * GLA_DV)
            outs.append(o_inter[hr] + jnp.dot(attb[hr], v[:, hv], preferred_element_type=F32))
            news.append(upd[hr, hv])
        o_ref[rows, :] = jnp.concatenate(outs, axis=1)
        s_ref[...] = dec * s_old + jnp.concatenate(news, axis=0)

    @pl.when(j == pl.num_programs(1) - 1)
    def _():
        sout_ref[0] = s_ref[...]


def _gla_prompt(gq, gk, gv, misc2d, w2, ba, n, s, tc):
    nj = s // tc
    mall, lmask, hmask = _gla_consts()
    row = lambda w: pl.BlockSpec((tc, w), lambda b, j: (b * nj + j, 0))
    return pl.pallas_call(
        functools.partial(_gla_kernel, nchunk=tc // GLA_CHUNK),
        grid=(n, nj),
        in_specs=[row(GLA_KW), row(GLA_KW), row(GLA_WIDTH), row(MISC_W),
                  _const_spec(w2.shape), _const_spec(ba.shape), _const_spec(mall.shape),
                  _const_spec(lmask.shape), _const_spec(hmask.shape)],
        out_specs=[row(GLA_WIDTH), pl.BlockSpec((1, GLA_KW, GLA_DV), lambda b, j: (b, 0, 0))],
        out_shape=[jax.ShapeDtypeStruct((n * s, GLA_WIDTH), F32),
                   jax.ShapeDtypeStruct((n, GLA_KW, GLA_DV), F32)],
        scratch_shapes=[pltpu.VMEM((GLA_KW, GLA_DV), F32)],
        compiler_params=_cparams(("parallel", "arbitrary")),
        name="gla_prompt",
    )(gq, gk, gv, misc2d, w2, ba, mall, lmask, hmask)


def _softmax_rows(s_parts, s_new):
    m = s_new
    for s in s_parts:
        m = jnp.maximum(m, jnp.max(s, axis=1, keepdims=True))
    p_parts = [jnp.exp(s - m) for s in s_parts]
    p_new = jnp.exp(s_new - m)
    l = p_new
    for p in p_parts:
        l = l + jnp.sum(p, axis=1, keepdims=True)
    return p_parts, p_new, 1.0 / jnp.maximum(l, 1e-30)


def _bf16_round(x):
    return x.astype(BF16).astype(F32)


_PAGE_ROWS = 4 * NSA_KVW


def _page_copies(cache_ref, pt_ref, buf, sem, seq, npages):
    return [pltpu.make_async_copy(cache_ref.at[pt_ref[seq * npages + p]],
                                  buf.at[pl.ds(p * _PAGE_ROWS, _PAGE_ROWS), :], sem)
            for p in range(npages)]


def _sample_consts(w_cmp, pe_cmp, npages, page):
    bpp = page // BLOCK
    z = jnp.zeros((BLOCK, NSA_HD, NSA_HD), w_cmp.dtype)
    per_c = []
    for c in range(2):
        w = w_cmp[c].transpose(1, 0, 2)
        rows = [jnp.concatenate([w if b2 == b1 else z for b2 in range(bpp)], axis=2)
                for b1 in range(bpp)]
        per_c.append(jnp.concatenate(rows, axis=1))
    w2t = jnp.stack(per_c).astype(BF16)
    pet = jnp.stack([jnp.tile(pe_cmp[c].T, (1, bpp)) for c in range(2)])
    pet = pet.reshape(2, NSA_HD, 1, page)
    j = np.arange(bpp * npages)
    col = np.arange(npages * page)
    expm = (j[:, None] == ((col % page) // BLOCK) * npages + col // page)
    return w2t, pet, jnp.asarray(expm.astype(np.float32), BF16)


def _sample_compute(buf, j, qc_ref, qbd_ref, gts_ref, kvn_ref, wkvn_ref, wcol_ref, win_ref,
                    w2t_ref, pet_ref, exp_ref, o_ref, wout_ref, *, npages, page):
    bpp = page // BLOCK
    nb = npages * bpp
    pos = npages * page
    cur = pos // BLOCK
    nh = NSA_HEADS
    dims_t = (((1,), (1,)), ((), ()))
    lane = lax.broadcasted_iota(jnp.int32, (nh, NSA_KVW), 1)
    qf = _bf16_round(qbd_ref[j] * (NSA_HD ** -0.5))
    qb = qf.astype(BF16)
    kvn = kvn_ref[j]
    wkvn = wkvn_ref[j]

    cmp_ = []
    for c in range(2):
        per_g = []
        for g in range(NSA_KV_HEADS):
            acc = jnp.zeros((npages, page), F32)
            for d in range(NSA_HD):
                r = c * NSA_KVW + g * NSA_HD + d
                x = (buf[pl.ds(r, npages, stride=_PAGE_ROWS), :] + pet_ref[c, d]).astype(BF16)
                acc += jnp.dot(x, w2t_ref[c, d], preferred_element_type=F32)
            per_g.append(acc)
        cmp_.append(jnp.concatenate(per_g, axis=1).astype(BF16))
    ck, cv = cmp_

    pidx = lax.broadcasted_iota(jnp.int32, (nh, npages), 1)
    sc, valid = [], []
    for bb in range(bpp):
        s = lax.dot_general((qc_ref[j, bb] * (NSA_HD ** -0.5)).astype(BF16), ck, dims_t,
                            preferred_element_type=F32)
        v = (bpp * pidx + bb + 1) * BLOCK - 1 <= pos
        sc.append(jnp.where(v, s, NEG))
        valid.append(v)
    mc = functools.reduce(jnp.maximum, [jnp.max(s, axis=1, keepdims=True) for s in sc])
    pc = [jnp.where(v, jnp.exp(s - mc), 0.0) for s, v in zip(sc, valid)]
    lc = functools.reduce(jnp.add, [jnp.sum(p, axis=1, keepdims=True) for p in pc])
    inv_c = 1.0 / jnp.maximum(lc, 1e-30)
    pc = [p * inv_c for p in pc]
    halves = []
    for g in range(NSA_KV_HEADS):
        t = None
        for bb in range(bpp):
            r = jnp.dot(pc[bb].astype(BF16), cv[:, g * NSA_KVW:(g + 1) * NSA_KVW],
                        preferred_element_type=F32)
            r = jnp.where((lane >= bb * BLOCK) & (lane < (bb + 1) * BLOCK), r, 0.0)
            t = r if t is None else t + r
        halves.append(t + pltpu.roll(t, BLOCK, axis=1))
    oc = jnp.where(lane < NSA_HD, halves[0], halves[1])

    width = -(-(nb + 1) // 128) * 128
    imp = [jnp.concatenate([jnp.broadcast_to(
        jnp.sum(p[g * NSA_HPG:(g + 1) * NSA_HPG], axis=0, keepdims=True), (NSA_HPG, npages))
        for g in range(NSA_KV_HEADS)], axis=0) for p in pc]
    imp = jnp.concatenate(imp + [jnp.zeros((nh, width - nb), F32)], axis=1)
    li = lax.broadcasted_iota(jnp.int32, (nh, width), 1)
    assert npages & (npages - 1) == 0
    shift = npages.bit_length() - 1
    bi = jnp.where(li < nb, bpp * jnp.bitwise_and(li, npages - 1)
                   + lax.shift_right_logical(li, shift), li)
    bf = bi.astype(F32)
    causal = bi <= cur
    forced = jnp.logical_and(causal, jnp.logical_or(bi == 0, bi > cur - N_LOCAL))
    score = jnp.where(forced, -4.0, jnp.where(causal, imp, -1.0))
    score = jnp.where(li <= nb, score, -3.0)

    def pick_one(_, c):
        score, sel = c
        mx = jnp.max(score, axis=1, keepdims=True)
        idx = jnp.min(jnp.where(score == mx, bf, float(2 * width)), axis=1, keepdims=True)
        pick = bf == idx
        return jnp.where(pick, -4.0, score), jnp.where(pick, 1.0, sel)

    if nb + 1 <= N_SELECT:
        sel = jnp.where(li <= nb, 1.0, 0.0)
    else:
        _, sel = lax.fori_loop(0, N_SELECT - 1 - N_LOCAL, pick_one,
                               (score, jnp.where(forced, 1.0, 0.0)))
    selb = jnp.where(sel > 0.0, 0.0, NEG)
    bias = jnp.dot(selb[:, :nb].astype(BF16), exp_ref[...], preferred_element_type=F32)
    selb_new = selb[:, nb:nb + 1]

    s_parts = []
    for p in range(npages):
        kt = buf[p * _PAGE_ROWS + 2 * NSA_KVW:p * _PAGE_ROWS + 3 * NSA_KVW, :].astype(BF16)
        s_parts.append(jnp.dot(qb, kt, preferred_element_type=F32)
                       + bias[:, p * page:(p + 1) * page])
    s_new = jnp.sum(qf * _bf16_round(kvn[:, 2 * NSA_KVW:3 * NSA_KVW]), axis=1,
                    keepdims=True) + selb_new
    p_parts, p_new, inv = _softmax_rows(s_parts, s_new)
    acc = _bf16_round(p_new) * _bf16_round(kvn[:, 3 * NSA_KVW:])
    for p in range(npages):
        vt = buf[p * _PAGE_ROWS + 3 * NSA_KVW:(p + 1) * _PAGE_ROWS, :].astype(BF16)
        acc += lax.dot_general(p_parts[p].astype(BF16), vt, dims_t, preferred_element_type=F32)
    os_ = acc * inv

    win = win_ref[j]
    wlen = win.shape[1]
    sw = jnp.dot(qb, win[:NSA_KVW].astype(BF16), preferred_element_type=F32)
    wpos = pos - wlen + lax.broadcasted_iota(jnp.int32, (nh, wlen), 1)
    dist = pos - wpos
    sw = jnp.where(jnp.logical_and(jnp.logical_and(dist >= 0, dist <= WINDOW), wpos >= 0),
                   sw, NEG)
    sw_new = jnp.sum(qf * _bf16_round(wkvn[:, :NSA_KVW]), axis=1, keepdims=True)
    (pw,), pw_new, inv_w = _softmax_rows([sw], sw_new)
    ow = (lax.dot_general(pw.astype(BF16), win[NSA_KVW:].astype(BF16), dims_t,
                          preferred_element_type=F32)
          + _bf16_round(pw_new) * _bf16_round(wkvn[:, NSA_KVW:])) * inv_w

    gt = _sigmoid(gts_ref[j])
    o_ref[j] = gt[:, 0:1] * oc + gt[:, 1:2] * os_ + gt[:, 2:3] * ow

    wl = lax.broadcasted_iota(jnp.int32, win.shape, 1)
    wout_ref[j] = jnp.where(wl == wlen - 1, wcol_ref[j], pltpu.roll(win, wlen - 1, axis=1))


def _nsa_sample_t_kernel(pt_ref, cache_ref, qc_ref, qbd_ref, gts_ref, kvn_ref, wkvn_ref, wcol_ref,
                         win_ref, w2t_ref, pet_ref, exp_ref, o_ref, wout_ref, buf0, buf1, sem,
                         *, npages, page):
    s = pl.program_id(0)
    compute = functools.partial(_sample_compute, qc_ref=qc_ref, qbd_ref=qbd_ref, gts_ref=gts_ref,
                                kvn_ref=kvn_ref, wkvn_ref=wkvn_ref, wcol_ref=wcol_ref,
                                win_ref=win_ref, w2t_ref=w2t_ref, pet_ref=pet_ref, exp_ref=exp_ref,
                                o_ref=o_ref, wout_ref=wout_ref, npages=npages, page=page)
    copies = lambda seq, buf, slot: _page_copies(cache_ref, pt_ref, buf, sem.at[slot], seq, npages)

    @pl.when(s == 0)
    def _():
        for c in copies(0, buf0, 0):
            c.start()

    for c in copies(2 * s + 1, buf1, 1):
        c.start()
    for c in copies(2 * s, buf0, 0):
        c.wait()
    compute(buf0, 0)

    @pl.when(s + 1 < pl.num_programs(0))
    def _():
        for c in copies(2 * s + 2, buf0, 0):
            c.start()

    for c in copies(2 * s + 1, buf1, 1):
        c.wait()
    compute(buf1, 1)


def _nsa_sample_t(page_table, cache_t, qc, qbd, gts3, kvn3, wkvn3, wcol, win_t, w2t, pet, expm):
    nd, npages = page_table.shape
    page = cache_t.shape[2]
    wlen = win_t.shape[2]
    assert nd % 2 == 0 and page % BLOCK == 0
    two = lambda *tail: pl.BlockSpec((2,) + tail, lambda s, pt: (s,) + (0,) * len(tail))
    const = lambda a: pl.BlockSpec(a.shape, lambda s, pt: (0,) * a.ndim,
                                   pipeline_mode=pl.Buffered(1))
    grid_spec = pltpu.PrefetchScalarGridSpec(
        num_scalar_prefetch=1,
        grid=(nd // 2,),
        in_specs=[pl.BlockSpec(memory_space=pl.ANY),
                  two(page // BLOCK, NSA_HEADS, 2 * NSA_KVW), two(NSA_HEADS, NSA_KVW),
                  two(NSA_HEADS, NSA_KVW), two(1, 4 * NSA_KVW), two(1, 2 * NSA_KVW),
                  two(2 * NSA_KVW, 1), two(2 * NSA_KVW, wlen), const(w2t), const(pet), const(expm)],
        out_specs=[two(NSA_HEADS, NSA_KVW), two(2 * NSA_KVW, wlen)],
        scratch_shapes=[pltpu.VMEM((npages * _PAGE_ROWS, page), F32),
                        pltpu.VMEM((npages * _PAGE_ROWS, page), F32),
                        pltpu.SemaphoreType.DMA((2,))])
    return pl.pallas_call(
        functools.partial(_nsa_sample_t_kernel, npages=npages, page=page),
        grid_spec=grid_spec,
        out_shape=[jax.ShapeDtypeStruct((nd, NSA_HEADS, NSA_KVW), F32),
                   jax.ShapeDtypeStruct(win_t.shape, F32)],
        compiler_params=_cparams(("arbitrary",)),
        name="nsa_sample",
    )(page_table.reshape(-1), cache_t, qc, qbd, gts3, kvn3, wkvn3, wcol, win_t, w2t, pet, expm)


def _gla_sample_kernel(gq_ref, gk_ref, gv_ref, misc_ref, w2_ref, ba_ref, s_ref, o_ref, sout_ref,
                       *, ns):
    la = _log_decay(misc_ref[...], w2_ref, ba_ref)
    pad = jnp.zeros((128 - 3 * ns, GLA_KW), F32)
    zt = jnp.concatenate([gq_ref[...] * (GLA_DK ** -0.5), gk_ref[...], jnp.exp(la), pad],
                         axis=0).T
    v = gv_ref[...]
    for i in range(ns):
        outs = []
        for h in range(GLA_HEADS):
            hr = slice(h * GLA_DK, (h + 1) * GLA_DK)
            qc = zt[hr, i:i + 1]
            kc = zt[hr, ns + i:ns + i + 1]
            ac = zt[hr, 2 * ns + i:2 * ns + i + 1]
            s_new = ac * s_ref[i, hr, :] + kc * v[i:i + 1, h * GLA_DV:(h + 1) * GLA_DV]
            sout_ref[i, hr, :] = s_new
            outs.append(jnp.sum(qc * s_new, axis=0, keepdims=True))
        o_ref[i:i + 1, :] = jnp.concatenate(outs, axis=1)


def _gla_sample(gq, gk, gv, misc2d, w2, ba, state3, ns=32):
    nd = gq.shape[0]
    row = lambda w: pl.BlockSpec((ns, w), lambda i: (i, 0))
    st = pl.BlockSpec((ns, GLA_KW, GLA_DV), lambda i: (i, 0, 0))
    return pl.pallas_call(
        functools.partial(_gla_sample_kernel, ns=ns),
        grid=(nd // ns,),
        in_specs=[row(GLA_KW), row(GLA_KW), row(GLA_WIDTH), row(MISC_W),
                  _const_spec(w2.shape), _const_spec(ba.shape), st],
        out_specs=[row(GLA_WIDTH), st],
        out_shape=[jax.ShapeDtypeStruct((nd, GLA_WIDTH), F32),
                   jax.ShapeDtypeStruct(state3.shape, F32)],
        compiler_params=_cparams(("parallel",)),
        name="gla_sample",
    )(gq, gk, gv, misc2d, w2, ba, state3)


def kernel(x_prompt, x_sample, cache_kv, cache_win, state_gla, page_table, p_prompt, p_sample,
           g_attn, w_in, w_cmp, pe_cmp, g_nsa_out, w_gla_a2, b_gla_a, g_gla_out, w_out, g_ffn,
           w_gate_up, w_down, g_ple, w_ple_gate, w_ple_proj, g_final):
    assert g_attn.shape[0] == 1, "single-layer trunk"
    n, s, d = x_prompt.shape
    nd, ds, _ = x_sample.shape
    assert ds == 1 and s % Q_BLOCK == 0 and s >= WINDOW
    w1 = _reorder_w_in(w_in[0])
    fw = _ffn_weights(g_nsa_out[0], g_gla_out[0], w_out[0], g_ffn[0], w_gate_up[0], w_down[0],
                      g_ple[0], w_ple_gate[0], w_ple_proj[0], g_final)
    cw2, cpe = _cmp_weights(w_cmp[0], pe_cmp[0])
    gw2, gba = _gla_gate_weights(w_gla_a2[0], b_gla_a[0])

    xp = x_prompt.reshape(n * s, d)
    q, kv, wkv, gq, gk, gv, gr, misc, kvt, wkvt = _project(xp, g_attn[0], w1, n, min(512, s))
    cmat, ksl, vst, kw, vwt = _nsa_prep(kv, wkv, kvt, wkvt, cw2, cpe, n, s, min(4096, s))
    o_n = _nsa_prompt(q, misc, cmat, ksl, vst, kw, vwt, n, s)
    o_g, s_p = _gla_prompt(gq, gk, gv, misc, gw2, gba, n, s, min(512, s))
    y_p = _mixer_ffn(o_n, o_g, gr, xp, p_prompt[0].reshape(n * s, -1), fw, min(512, n * s))

    xs = x_sample.reshape(nd, d)
    qs, kvs, wkvs, gqs, gks, gvs, grs, miscs, kvts, _ = _project(xs, g_attn[0], w1, 1, nd)
    page = cache_kv.shape[2]
    npages = page_table.shape[1]
    cache_t = _to_channel_major(cache_kv[0])
    win_t = _to_channel_major(cache_win[0])
    q4 = qs.reshape(nd, NSA_KV_HEADS, NSA_HPG, NSA_HD)
    qbd = jnp.concatenate([_place(q4[:, g], g, NSA_KV_HEADS) for g in range(NSA_KV_HEADS)],
                          axis=1)
    bpp = page // BLOCK
    qc = jnp.stack([jnp.concatenate([_place(q4[:, g], g * bpp + bb, NSA_KV_HEADS * bpp)
                                     for g in range(NSA_KV_HEADS)], axis=1)
                    for bb in range(bpp)], axis=1)
    gts3 = jnp.pad(miscs[:, :N_GATES].reshape(nd, NSA_HEADS, 3), ((0, 0), (0, 0), (0, NSA_KVW - 3)))
    w2t, pet, expm = _sample_consts(w_cmp[0], pe_cmp[0], npages, page)
    o8, win_new = _nsa_sample_t(page_table, cache_t, qc, qbd, gts3, kvs.reshape(nd, 1, -1),
                                wkvs.reshape(nd, 1, -1), wkvs.reshape(nd, -1, 1), win_t,
                                w2t, pet, expm)
    o8 = o8.reshape(nd, NSA_KV_HEADS, NSA_HPG, NSA_KV_HEADS, NSA_HD)
    o_ns = jnp.stack([o8[:, 0, :, 0], o8[:, 1, :, 1]], axis=1).reshape(nd, NSA_WIDTH)
    o_gs, s_s = _gla_sample(gqs, gks, gvs, miscs, gw2, gba,
                            state_gla[0].reshape(nd, GLA_KW, GLA_DV))
    y_s = _mixer_ffn(o_ns, o_gs, grs, xs, p_sample[0].reshape(nd, -1), fw, nd)

    wkeep = min(WINDOW, s)
    return (y_p.reshape(n, s, d),
            y_s.reshape(nd, 1, d),
            _from_channel_major(kvt, 4)[None],
            _from_channel_major(wkvt[:, :, s - wkeep:], 2)[None],
            s_p.reshape(1, n, GLA_HEADS, GLA_DK, GLA_DV),
            _from_channel_major(kvts, 4).reshape(1, nd, 1, 4, NSA_KV_HEADS, NSA_HD),
            _from_channel_major(win_new, 2)[None],
            s_s.reshape(1, nd, GLA_HEADS, GLA_DK, GLA_DV))


def _to_channel_major(x):
    n, t = x.shape[:2]
    return jnp.transpose(x, (0, 2, 3, 4, 1)).reshape(n, -1, t)


def _from_channel_major(xt, c):
    n, _, t = xt.shape
    return jnp.transpose(xt.reshape(n, c, NSA_KV_HEADS, NSA_HD, t), (0, 4, 1, 2, 3))


def _place(x, slot, nslots):
    z = jnp.zeros_like(x)
    return jnp.concatenate([x if i == slot else z for i in range(nslots)], axis=-1)
```

```python
import functools

import numpy as np
import jax
import jax.numpy as jnp
from jax import lax
from jax.experimental import pallas as pl
from jax.experimental.pallas import tpu as pltpu

F32 = jnp.float32
BF16 = jnp.bfloat16

EPS = 1e-6
NEG = -1e30
LOG2E = 1.4426950408889634

NSA_HEADS = 8
NSA_KV_HEADS = 2
NSA_HPG = NSA_HEADS // NSA_KV_HEADS
NSA_HD = 64
NSA_WIDTH = NSA_HEADS * NSA_HD
NSA_KVW = NSA_KV_HEADS * NSA_HD
BLOCK = 64
N_SELECT = 16
N_LOCAL = 2
WINDOW = 512
Q_BLOCK = 128
FORCE_SCORE = float(NSA_HPG + 1)
GLA_HEADS = 4
GLA_DK = 64
GLA_DV = 128
GLA_WIDTH = GLA_HEADS * GLA_DV
GLA_KW = GLA_HEADS * GLA_DK
GLA_GATE_RANK = 16
GLA_GATE_TEMP = 16.0
GLA_CHUNK = 64
N_GATES = 3 * NSA_HEADS
MISC_W = 128

VMEM_LIMIT = 56 * 1024 * 1024


def _cparams(sem):
    return pltpu.CompilerParams(dimension_semantics=sem, vmem_limit_bytes=VMEM_LIMIT)


def _const_spec(shape):
    nd = len(shape)
    return pl.BlockSpec(shape, lambda *_: (0,) * nd, pipeline_mode=pl.Buffered(1))


def _sigmoid(x):
    return 1.0 / (1.0 + jnp.exp(-x))


def _rms(x, g):
    ms = jnp.mean(x * x, axis=-1, keepdims=True)
    return x * lax.rsqrt(ms + EPS) * g


def _split3(x):
    hi = x.astype(BF16)
    r = x - hi.astype(F32)
    mid = r.astype(BF16)
    lo = (r - mid.astype(F32)).astype(BF16)
    return hi, mid, lo


_P_Q, _P_KV, _P_WKV, _P_GQ, _P_GK, _P_GV, _P_GR, _P_MISC, _P_END = (
    0, 512, 1024, 1280, 1536, 1792, 2304, 2816, 2944)


def _reorder_w_in(w_in):
    offs = np.cumsum([0, NSA_WIDTH, 4 * NSA_KVW, 2 * NSA_KVW, N_GATES, GLA_KW, GLA_KW,
                      GLA_WIDTH, GLA_GATE_RANK, GLA_WIDTH])
    q, kv, wkv, gts, gq, gk, gv, ga, gr = [w_in[:, offs[i]:offs[i + 1]] for i in range(9)]
    pad = jnp.zeros((w_in.shape[0], MISC_W - N_GATES - GLA_GATE_RANK), w_in.dtype)
    return jnp.concatenate([q, kv, wkv, gq, gk, gv, gr, gts, ga, pad], axis=1).astype(BF16)


def _proj_kernel(x_ref, g_ref, w_ref, wt_ref, q_ref, kv_ref, wkv_ref, gq_ref, gk_ref, gv_ref,
                 gr_ref, misc_ref, kvt_ref, wkvt_ref):
    xn = _rms(x_ref[...], g_ref[...]).astype(BF16)
    for ref, a, b in ((q_ref, _P_Q, _P_KV), (kv_ref, _P_KV, _P_WKV), (wkv_ref, _P_WKV, _P_GQ),
                      (gq_ref, _P_GQ, _P_GK), (gk_ref, _P_GK, _P_GV), (gv_ref, _P_GV, _P_GR),
                      (gr_ref, _P_GR, _P_MISC), (misc_ref, _P_MISC, _P_END)):
        ref[...] = jnp.dot(xn, w_ref[:, a:b], preferred_element_type=F32)
    dims_t = (((1,), (1,)), ((), ()))
    nkv = kvt_ref.shape[1]
    kvt_ref[0] = lax.dot_general(wt_ref[0:nkv, :], xn, dims_t, preferred_element_type=F32)
    wkvt_ref[0] = lax.dot_general(wt_ref[nkv:, :], xn, dims_t, preferred_element_type=F32)


def _project(x2d, g_attn, w1, n, tm):
    t, d = x2d.shape
    s = t // n
    nj = s // tm
    widths = (512, 512, 256, 256, 256, 512, 512, MISC_W)
    wt = w1[:, _P_KV:_P_GQ].T
    tspec = lambda rows: pl.BlockSpec((1, rows, tm), lambda i: (i // nj, 0, i % nj))
    return pl.pallas_call(
        _proj_kernel,
        grid=(t // tm,),
        in_specs=[pl.BlockSpec((tm, d), lambda i: (i, 0)),
                  _const_spec((1, d)),
                  _const_spec(w1.shape),
                  _const_spec(wt.shape)],
        out_specs=[pl.BlockSpec((tm, w), lambda i: (i, 0)) for w in widths]
                  + [tspec(4 * NSA_KVW), tspec(2 * NSA_KVW)],
        out_shape=[jax.ShapeDtypeStruct((t, w), F32) for w in widths]
                  + [jax.ShapeDtypeStruct((n, 4 * NSA_KVW, s), F32),
                     jax.ShapeDtypeStruct((n, 2 * NSA_KVW, s), F32)],
        compiler_params=_cparams(("parallel",)),
        name="proj",
    )(x2d, g_attn.reshape(1, d), w1, wt)


def _group_mean_sq(x, bd_ref, width):
    hi, mid, lo = _split3(x * x)
    bd = bd_ref[...]
    s = (jnp.dot(hi, bd, preferred_element_type=F32) + jnp.dot(mid, bd, preferred_element_type=F32)
         + jnp.dot(lo, bd, preferred_element_type=F32))
    return s * (1.0 / width)


def _ffn_kernel(on_ref, og_ref, gr_ref, x_ref, p_ref, gn_ref, gg_ref, bd64_ref, bd128_ref,
                wo_ref, gffn_ref, wgu_ref, wd_ref, gple_ref, wpg_ref, wpp_ref, gfin_ref,
                y_ref, h_ref, acc_ref, *, n_ff):
    o_n = on_ref[...]
    o_g = og_ref[...]
    r = gr_ref[...]
    a_n = o_n * lax.rsqrt(_group_mean_sq(o_n, bd64_ref, NSA_HD) + EPS) * gn_ref[...]
    a_g = o_g * lax.rsqrt(_group_mean_sq(o_g, bd128_ref, GLA_DV) + EPS) * gg_ref[...]
    a_g = a_g * (r * _sigmoid(r))
    x1 = (x_ref[...]
          + jnp.dot(a_n.astype(BF16), wo_ref[0:NSA_WIDTH, :], preferred_element_type=F32)
          + jnp.dot(a_g.astype(BF16), wo_ref[NSA_WIDTH:, :], preferred_element_type=F32))
    h_ref[...] = _rms(x1, gffn_ref[...]).astype(BF16)
    acc_ref[...] = x1

    def body(c, carry):
        gu = jnp.dot(h_ref[...], wgu_ref[c], preferred_element_type=F32)
        half = gu.shape[1] // 2
        gt = gu[:, :half]
        act = (gt * _sigmoid(gt) * gu[:, half:]).astype(BF16)
        acc_ref[...] += jnp.dot(act, wd_ref[c], preferred_element_type=F32)
        return carry

    lax.fori_loop(0, n_ff, body, 0)
    x2 = acc_ref[...]
    gate = _sigmoid(jnp.dot(_rms(x2, gple_ref[...]).astype(BF16), wpg_ref[...],
                            preferred_element_type=F32))
    x3 = x2 + jnp.dot(p_ref[...].astype(BF16), wpp_ref[...], preferred_element_type=F32) * gate
    y_ref[...] = _rms(x3, gfin_ref[...])


def _block_diag_ones(n, width):
    idx = np.arange(n) // width
    return jnp.asarray((idx[:, None] == idx[None, :]).astype(np.float32), BF16)


def _ffn_weights(g_nsa_out, g_gla_out, w_out, g_ffn, w_gate_up, w_down, g_ple, w_ple_gate,
                 w_ple_proj, g_final, ff_chunk=256):
    d = w_out.shape[1]
    d_ff = w_down.shape[0]
    n_ff = d_ff // ff_chunk
    wg = w_gate_up[:, :d_ff].reshape(d, n_ff, ff_chunk)
    wu = w_gate_up[:, d_ff:].reshape(d, n_ff, ff_chunk)
    wgu = jnp.concatenate([wg, wu], axis=2).transpose(1, 0, 2).astype(BF16)
    wd = w_down.reshape(n_ff, ff_chunk, d).astype(BF16)
    return dict(gn=g_nsa_out.reshape(1, NSA_WIDTH), gg=g_gla_out.reshape(1, GLA_WIDTH),
                bd64=_block_diag_ones(NSA_WIDTH, NSA_HD), bd128=_block_diag_ones(GLA_WIDTH, GLA_DV),
                wo=w_out.astype(BF16), gffn=g_ffn.reshape(1, d), wgu=wgu, wd=wd,
                gple=g_ple.reshape(1, d), wpg=w_ple_gate.astype(BF16),
                wpp=w_ple_proj.astype(BF16), gfin=g_final.reshape(1, d))


def _mixer_ffn(o_n, o_g, gr, x2d, p2d, fw, tm):
    t, d = x2d.shape
    n_ff = fw["wgu"].shape[0]
    consts = [fw[k] for k in ("gn", "gg", "bd64", "bd128", "wo", "gffn", "wgu", "wd", "gple",
                              "wpg", "wpp", "gfin")]
    row = lambda w: pl.BlockSpec((tm, w), lambda i: (i, 0))
    return pl.pallas_call(
        functools.partial(_ffn_kernel, n_ff=n_ff),
        grid=(t // tm,),
        in_specs=[row(NSA_WIDTH), row(GLA_WIDTH), row(GLA_WIDTH), row(d), row(p2d.shape[1])]
                 + [_const_spec(c.shape) for c in consts],
        out_specs=row(d),
        out_shape=jax.ShapeDtypeStruct((t, d), F32),
        scratch_shapes=[pltpu.VMEM((tm, d), BF16), pltpu.VMEM((tm, d), F32)],
        compiler_params=_cparams(("parallel",)),
        name="mixer_ffn",
    )(o_n, o_g, gr, x2d, p2d, *consts)


def _cmp_weights(w_cmp, pe_cmp):
    z = jnp.zeros((BLOCK, NSA_HD, NSA_HD), w_cmp.dtype)

    def bd(w):
        return jnp.concatenate([jnp.concatenate([w, z], axis=2),
                                jnp.concatenate([z, w], axis=2)], axis=1)

    w2 = jnp.stack([bd(w_cmp[0]), bd(w_cmp[1])], axis=1).astype(BF16)
    pe = jnp.stack([jnp.concatenate([pe_cmp[0], pe_cmp[0]], axis=1),
                    jnp.concatenate([pe_cmp[1], pe_cmp[1]], axis=1)], axis=1)
    return w2, pe.reshape(BLOCK, 2, 1, NSA_KVW)


def _compress(xk_ref, xv_ref, w2_ref, pe_ref, nb):
    acc_k = jnp.zeros((nb, NSA_KVW), F32)
    acc_v = jnp.zeros((nb, NSA_KVW), F32)
    for l in range(BLOCK):
        xk = (xk_ref[pl.ds(l, nb, stride=BLOCK), :] + pe_ref[l, 0]).astype(BF16)
        xv = (xv_ref[pl.ds(l, nb, stride=BLOCK), :] + pe_ref[l, 1]).astype(BF16)
        acc_k += jnp.dot(xk, w2_ref[l, 0], preferred_element_type=F32)
        acc_v += jnp.dot(xv, w2_ref[l, 1], preferred_element_type=F32)
    return acc_k, acc_v


def _prep_kernel(ckin_ref, cvin_ref, ksin_ref, kwin_ref, vst_in_ref, vwt_in_ref, w2_ref, pe_ref,
                 c_ref, ks_ref, vst_ref, kw_ref, vwt_ref, *, tk):
    nb = tk // BLOCK
    ck, cv = _compress(ckin_ref, cvin_ref, w2_ref, pe_ref, nb)
    c_ref[:, 0:NSA_KVW] = ck
    c_ref[:, NSA_KVW:] = cv
    ks_ref[...] = ksin_ref[...].astype(BF16)
    kw_ref[...] = kwin_ref[...].astype(BF16)
    vst_ref[...] = vst_in_ref[...].astype(BF16)
    vwt_ref[...] = vwt_in_ref[...].astype(BF16)


def _nsa_prep(kv2d, wkv2d, kvt, wkvt, w2, pe, n, s, tk):
    t = n * s
    nj = s // tk
    kv_specs = [pl.BlockSpec((tk, NSA_KVW), (lambda b, j, c=c: (b * nj + j, c))) for c in range(3)]
    return pl.pallas_call(
        functools.partial(_prep_kernel, tk=tk),
        grid=(n, nj),
        in_specs=kv_specs + [pl.BlockSpec((tk, NSA_KVW), lambda b, j: (b * nj + j, 0)),
                             pl.BlockSpec((1, NSA_KVW, tk), lambda b, j: (b, 3, j)),
                             pl.BlockSpec((1, NSA_KVW, tk), lambda b, j: (b, 1, j)),
                             _const_spec(w2.shape), _const_spec(pe.shape)],
        out_specs=[pl.BlockSpec((tk // BLOCK, 2 * NSA_KVW), lambda b, j: (b * nj + j, 0)),
                   pl.BlockSpec((tk, NSA_KVW), lambda b, j: (b * nj + j, 0)),
                   pl.BlockSpec((1, NSA_KVW, tk), lambda b, j: (b, 0, j)),
                   pl.BlockSpec((tk, NSA_KVW), lambda b, j: (b * nj + j, 0)),
                   pl.BlockSpec((1, NSA_KVW, tk), lambda b, j: (b, 0, j))],
        out_shape=[jax.ShapeDtypeStruct((t // BLOCK, 2 * NSA_KVW), F32),
                   jax.ShapeDtypeStruct((t, NSA_KVW), BF16),
                   jax.ShapeDtypeStruct((n, NSA_KVW, s), BF16),
                   jax.ShapeDtypeStruct((t, NSA_KVW), BF16),
                   jax.ShapeDtypeStruct((n, NSA_KVW, s), BF16)],
        compiler_params=_cparams(("parallel", "parallel")),
        name="nsa_prep",
    )(kv2d, kv2d, kv2d, wkv2d, kvt, wkvt, w2, pe)


_LQ = NSA_KV_HEADS * NSA_HPG * Q_BLOCK
_LG = NSA_HPG * Q_BLOCK
_SEL_GROUP = 4 * Q_BLOCK


def _rep_heads(x):
    a, b = x[:, :Q_BLOCK], x[:, Q_BLOCK:]
    return jnp.concatenate([a] * NSA_HPG + [b] * NSA_HPG, axis=1)


def _flash_update(s, vt, carry):
    m, l, acc0, acc1 = carry
    m_new = jnp.maximum(m, jnp.max(s, axis=0, keepdims=True))
    alpha = jnp.exp2(m - m_new)
    p = jnp.exp2(s - m_new)
    l = alpha * l + jnp.sum(p, axis=0, keepdims=True)
    pb = p.astype(BF16)
    acc0 = alpha[:, :_LG] * acc0 + jnp.dot(vt[:NSA_HD], pb[:, :_LG], preferred_element_type=F32)
    acc1 = alpha[:, _LG:] * acc1 + jnp.dot(vt[NSA_HD:], pb[:, _LG:], preferred_element_type=F32)
    return m_new, l, acc0, acc1


def _flash_init():
    return (jnp.full((1, _LQ), NEG, F32), jnp.zeros((1, _LQ), F32),
            jnp.zeros((NSA_HD, _LG), F32), jnp.zeros((NSA_HD, _LG), F32))


def _flash_out(carry):
    m, l, acc0, acc1 = carry
    inv = 1.0 / jnp.maximum(l, 1e-30)
    return acc0 * inv[:, :_LG], acc1 * inv[:, _LG:]


def _select_blocks(imp, cur, nbs):
    if nbs <= N_SELECT:
        return jnp.ones(imp.shape, F32)
    bi = lax.broadcasted_iota(jnp.int32, imp.shape, 0)
    bf = bi.astype(F32)
    causal = bi <= cur
    forced = jnp.logical_and(causal, jnp.logical_or(bi == 0, bi > cur - N_LOCAL))
    score = jnp.where(forced, -2.0, jnp.where(causal, imp, -1.0))

    def body(_, c):
        score, sel = c
        mx = jnp.max(score, axis=0, keepdims=True)
        idx = jnp.min(jnp.where(score == mx, bf, float(nbs)), axis=0, keepdims=True)
        pick = bf == idx
        return jnp.where(pick, -2.0, score), jnp.where(pick, 1.0, sel)

    _, sel = lax.fori_loop(0, N_SELECT - 1 - N_LOCAL, body,
                           (score, jnp.where(forced, 1.0, 0.0)))
    return sel


def _nsa_prompt_kernel(q_ref, misc_ref, c_ref, ks_ref, vst_ref, ind_ref, indw_ref, *rest, nbc):
    kw_refs = rest[0:5]
    vwt_refs = rest[5:10]
    o_ref = rest[10]
    selb_ref, s0_ref, s1_ref, p0_ref, p1_ref = rest[11:16]
    i = pl.program_id(1)

    qT = (q_ref[...] * (NSA_HD ** -0.5 * LOG2E)).T
    z = jnp.zeros((NSA_HD, Q_BLOCK), F32)
    top = [qT[NSA_HD * h:NSA_HD * (h + 1)] for h in range(NSA_HPG)] + [z] * NSA_HPG
    bot = [z] * NSA_HPG + [qT[NSA_HD * h:NSA_HD * (h + 1)] for h in range(NSA_HPG, NSA_HEADS)]
    qt = jnp.concatenate([jnp.concatenate(top, axis=1), jnp.concatenate(bot, axis=1)],
                         axis=0).astype(BF16)

    cmat = c_ref[...]
    ck = cmat[:, :NSA_KVW].astype(BF16)
    cv = cmat[:, NSA_KVW:].astype(BF16)
    sc = jnp.dot(ck, qt, preferred_element_type=F32)
    b_io = lax.broadcasted_iota(jnp.int32, (nbc, _LQ), 0)
    qoff = jnp.bitwise_and(lax.broadcasted_iota(jnp.int32, (nbc, _LQ), 1), Q_BLOCK - 1)
    valid = b_io * BLOCK + (BLOCK - 1) <= i * Q_BLOCK + qoff
    sc = jnp.where(valid, sc, NEG)
    mc = jnp.max(sc, axis=0, keepdims=True)
    pc = jnp.where(valid, jnp.exp2(sc - mc), 0.0)
    pc = pc / jnp.maximum(jnp.sum(pc, axis=0, keepdims=True), 1e-30)
    oc_full = lax.dot_general(cv, pc.astype(BF16), (((0,), (0,)), ((), ())),
                              preferred_element_type=F32)
    oc = (oc_full[:NSA_HD, :_LG], oc_full[NSA_HD:, _LG:])

    qo = lax.broadcasted_iota(jnp.int32, (1, Q_BLOCK), 1)
    cur = 2 * i + (qo >= BLOCK).astype(jnp.int32)
    for g in range(NSA_KV_HEADS):
        imp = pc[:, g * _LG:g * _LG + Q_BLOCK]
        for hh in range(1, NSA_HPG):
            imp = imp + pc[:, g * _LG + hh * Q_BLOCK:g * _LG + (hh + 1) * Q_BLOCK]
        sel = _select_blocks(imp, cur, nbc)
        selb_ref[:, g * Q_BLOCK:(g + 1) * Q_BLOCK] = jnp.where(sel > 0.0, 0.0, NEG)

    nbg = _SEL_GROUP // BLOCK
    ind = ind_ref[...]
    zpad = jnp.zeros((NSA_KVW - 16, _LQ), BF16)

    def scores(jg):
        off = pl.multiple_of(jg * _SEL_GROUP, _SEL_GROUP)
        ka = jnp.concatenate([ks_ref[pl.ds(off, _SEL_GROUP), :], ind], axis=1)
        brow = selb_ref[pl.ds(pl.multiple_of(jg * nbg, nbg), nbg), :]
        baug = jnp.concatenate([_rep_heads(brow), jnp.zeros((16 - nbg, _LQ), F32)],
                               axis=0).astype(BF16)
        return jnp.dot(ka, jnp.concatenate([qt, baug, zpad], axis=0), preferred_element_type=F32)

    def values(jg, p_ref):
        vt = vst_ref[0, :, pl.ds(pl.multiple_of(jg * _SEL_GROUP, _SEL_GROUP), _SEL_GROUP)]
        return (jnp.dot(vt[:NSA_HD], p_ref[:, :_LG], preferred_element_type=F32),
                jnp.dot(vt[NSA_HD:], p_ref[:, _LG:], preferred_element_type=F32))

    def stage(jg, s_cur, p_cur, s_nxt, p_prev, st, diagonal=False):
        m, l, b0, b1 = st
        if s_nxt is not None:
            s_nxt[...] = scores(jg + 1)
        pv0, pv1 = values(jnp.maximum(jg - 1, 0), p_prev)
        s = s_cur[...]
        if diagonal:
            rowg = lax.broadcasted_iota(jnp.int32, (_SEL_GROUP, _LQ), 0)
            qcg = jnp.bitwise_and(lax.broadcasted_iota(jnp.int32, (_SEL_GROUP, _LQ), 1),
                                  Q_BLOCK - 1)
            s = jnp.where(jg * _SEL_GROUP + rowg <= i * Q_BLOCK + qcg, s, NEG)
        m_new = jnp.maximum(m, jnp.max(s, axis=0, keepdims=True))
        alpha = jnp.exp2(m - m_new)
        p = jnp.exp2(s - m_new)
        p_cur[...] = p.astype(BF16)
        l = alpha * l + jnp.sum(p, axis=0, keepdims=True)
        return m_new, l, alpha[:, :_LG] * (b0 + pv0), alpha[:, _LG:] * (b1 + pv1)

    def finish(st, p_last, jg):
        m, l, b0, b1 = st
        pv0, pv1 = values(jg, p_last)
        inv = 1.0 / jnp.maximum(l, 1e-30)
        return (b0 + pv0) * inv[:, :_LG], (b1 + pv1) * inv[:, _LG:]

    nfull = lax.shift_right_logical(i, 2)
    s0_ref[...] = scores(0)
    p1_ref[...] = jnp.zeros(p1_ref.shape, BF16)

    def pair(t, st):
        st = stage(2 * t, s0_ref, p0_ref, s1_ref, p1_ref, st)
        return stage(2 * t + 1, s1_ref, p1_ref, s0_ref, p0_ref, st)

    st = lax.fori_loop(0, lax.shift_right_logical(nfull, 1), pair, _flash_init())

    def odd_tail(st):
        st = stage(nfull - 1, s0_ref, p0_ref, s1_ref, p1_ref, st)
        st = stage(nfull, s1_ref, p1_ref, None, p0_ref, st, diagonal=True)
        return finish(st, p1_ref, nfull)

    def even_tail(st):
        st = stage(nfull, s0_ref, p0_ref, None, p1_ref, st, diagonal=True)
        return finish(st, p0_ref, nfull)

    os_ = lax.cond(jnp.bitwise_and(nfull, 1) == 1, odd_tail, even_tail, st)

    nwt = len(kw_refs)
    row = lax.broadcasted_iota(jnp.int32, (Q_BLOCK, _LQ), 0)
    qcol = jnp.bitwise_and(lax.broadcasted_iota(jnp.int32, (Q_BLOCK, _LQ), 1), Q_BLOCK - 1)
    tile = lax.broadcasted_iota(jnp.int32, (16, _LQ), 0)
    wbias = jnp.where(tile < nwt - 1 - i, NEG, 0.0).astype(BF16)
    kwa = jnp.concatenate([jnp.concatenate([r[...] for r in kw_refs], axis=0), indw_ref[...]],
                          axis=1)
    sw = jnp.dot(kwa, jnp.concatenate([qt, wbias, zpad], axis=0), preferred_element_type=F32)
    sw = jnp.concatenate([jnp.where(row >= qcol, sw[:Q_BLOCK], NEG),
                          sw[Q_BLOCK:(nwt - 1) * Q_BLOCK],
                          jnp.where(row <= qcol, sw[(nwt - 1) * Q_BLOCK:], NEG)], axis=0)
    ow = _flash_out(_flash_update(sw, jnp.concatenate([r[0] for r in vwt_refs], axis=1),
                                  _flash_init()))

    gt = _sigmoid(misc_ref[...]).T
    outs = []
    for g in range(NSA_KV_HEADS):
        for hh in range(NSA_HPG):
            h = g * NSA_HPG + hh
            sl = slice(hh * Q_BLOCK, (hh + 1) * Q_BLOCK)
            outs.append(gt[3 * h:3 * h + 1] * oc[g][:, sl] + gt[3 * h + 1:3 * h + 2] * os_[g][:, sl]
                        + gt[3 * h + 2:3 * h + 3] * ow[g][:, sl])
    o_ref[...] = jnp.concatenate(outs, axis=0).T


def _nsa_prompt(q2d, misc2d, cmat, ks, vst, kw, vwt, n, s):
    assert s % _SEL_GROUP == 0
    nq = s // Q_BLOCK
    nbc = s // BLOCK
    blk = np.arange(_SEL_GROUP) // BLOCK
    ind = jnp.asarray((blk[:, None] == np.arange(NSA_KVW)[None, :]).astype(np.float32), BF16)
    wtile = np.arange(5 * Q_BLOCK) // Q_BLOCK
    indw = jnp.asarray((wtile[:, None] == np.arange(NSA_KVW)[None, :]).astype(np.float32), BF16)
    wk_specs = [pl.BlockSpec((Q_BLOCK, NSA_KVW),
                             (lambda b, i, t=t: (b * nq + jnp.maximum(i - 4 + t, 0), 0)))
                for t in range(5)]
    wv_specs = [pl.BlockSpec((1, NSA_KVW, Q_BLOCK),
                             (lambda b, i, t=t: (b, 0, jnp.maximum(i - 4 + t, 0))))
                for t in range(5)]
    return pl.pallas_call(
        functools.partial(_nsa_prompt_kernel, nbc=nbc),
        grid=(n, nq),
        in_specs=[pl.BlockSpec((Q_BLOCK, NSA_WIDTH), lambda b, i: (b * nq + i, 0)),
                  pl.BlockSpec((Q_BLOCK, MISC_W), lambda b, i: (b * nq + i, 0)),
                  pl.BlockSpec((nbc, 2 * NSA_KVW), lambda b, i: (b, 0)),
                  pl.BlockSpec((s, NSA_KVW), lambda b, i: (b, 0)),
                  pl.BlockSpec((1, NSA_KVW, s), lambda b, i: (b, 0, 0)),
                  _const_spec(ind.shape), _const_spec(indw.shape)] + wk_specs + wv_specs,
        out_specs=pl.BlockSpec((Q_BLOCK, NSA_WIDTH), lambda b, i: (b * nq + i, 0)),
        out_shape=jax.ShapeDtypeStruct((n * s, NSA_WIDTH), F32),
        scratch_shapes=[pltpu.VMEM((nbc, 2 * Q_BLOCK), F32),
                        pltpu.VMEM((_SEL_GROUP, _LQ), F32), pltpu.VMEM((_SEL_GROUP, _LQ), F32),
                        pltpu.VMEM((_SEL_GROUP, _LQ), BF16), pltpu.VMEM((_SEL_GROUP, _LQ), BF16)],
        compiler_params=_cparams(("parallel", "arbitrary")),
        name="nsa_prompt",
    )(q2d, misc2d, cmat, ks, vst, ind, indw, *([kw] * 5), *([vwt] * 5))


_N_LEVELS = int(np.log2(GLA_CHUNK))


def _gla_consts():
    c = GLA_CHUNK
    t = np.arange(c)
    mats = [t[None, :] <= t[:, None], t[None, :] > t[:, None]]
    masks = []
    w = c // 2
    while w >= 1:
        blk = t // (2 * w)
        mid = blk * 2 * w + w
        upper = t >= mid
        m = np.zeros((c, c), bool)
        for r in range(c):
            if upper[r]:
                m[r, mid[r]:r + 1] = True
            else:
                m[r, r + 1:mid[r]] = True
        mats.append(m)
        masks.append((blk[:, None] == blk[None, :]) & upper[:, None] & ~upper[None, :])
        w //= 2
    masks.append(np.eye(c, dtype=bool))
    mall = jnp.asarray(np.concatenate(mats, axis=0).astype(np.float32), BF16)
    lmask = jnp.asarray(np.tile(np.stack(masks).astype(np.float32), (1, GLA_HEADS, 1)))
    hd = np.arange(GLA_KW) // GLA_DK
    hmask = jnp.asarray((hd[:, None] == hd[None, :]).astype(np.float32))
    return mall, lmask, hmask


def _gla_gate_weights(w_a2, b_a):
    w2 = jnp.zeros((MISC_W, GLA_KW), F32).at[N_GATES:N_GATES + GLA_GATE_RANK].set(w_a2)
    return w2.astype(BF16), b_a.reshape(1, GLA_KW)


def _log_decay(misc, w2_ref, ba_ref):
    x = jnp.dot(misc.astype(BF16), w2_ref[...], preferred_element_type=F32) + ba_ref[...]
    return (jnp.minimum(x, 0.0) - jnp.log1p(jnp.exp(-jnp.abs(x)))) * (1.0 / GLA_GATE_TEMP)


def _gla_kernel(gq_ref, gk_ref, gv_ref, misc_ref, w2_ref, ba_ref, mall_ref, lmask_ref, hm_ref,
                o_ref, sout_ref, s_ref, *, nchunk):
    j = pl.program_id(1)

    @pl.when(j == 0)
    def _():
        s_ref[...] = jnp.zeros(s_ref.shape, F32)

    la_all = _log_decay(misc_ref[...], w2_ref, ba_ref)
    mall = mall_ref[...]
    hm = hm_ref[...]
    c = GLA_CHUNK
    dims_t = (((1,), (1,)), ((), ()))

    def stack(x):
        return (jnp.concatenate([x] * GLA_HEADS, axis=0) * hm).astype(BF16)

    for ci in range(nchunk):
        rows = slice(ci * c, (ci + 1) * c)
        q = gq_ref[rows, :] * (GLA_DK ** -0.5)
        k = gk_ref[rows, :]
        v = gv_ref[rows, :].astype(BF16)
        la = la_all[rows]
        hi, mid, lo = _split3(la)
        ex = jnp.exp(jnp.dot(mall, hi, preferred_element_type=F32)
                     + jnp.dot(mall, mid, preferred_element_type=F32)
                     + jnp.dot(mall, lo, preferred_element_type=F32))
        att = lax.dot_general(stack(q), k.astype(BF16), dims_t,
                              preferred_element_type=F32) * lmask_ref[_N_LEVELS]
        for lev in range(_N_LEVELS):
            e = ex[(2 + lev) * c:(3 + lev) * c]
            att += lax.dot_general(stack(q * e), (k * e).astype(BF16), dims_t,
                                   preferred_element_type=F32) * lmask_ref[lev]
        attb = att.astype(BF16)
        s_old = s_ref[...]
        o_inter = jnp.dot(stack(q * ex[0:c]), s_old.astype(BF16), preferred_element_type=F32)
        kd = (k * ex[c:2 * c]).astype(BF16)
        upd = lax.dot_general(kd, v, (((0,), (0,)), ((), ())), preferred_element_type=F32)
        dec = jnp.exp(jnp.sum(la.T, axis=1, keepdims=True))
        outs, news = [], []
        for h in range(GLA_HEADS):
            hr = slice(h * GLA_DK, (h + 1) * GLA_DK)
            hv = slice(h * GLA_DV, (h + 1) * GLA_DV)
            outs.append(o_inter[hr] + jnp.dot(attb[hr], v[:, hv], preferred_element_type=F32))
            news.append(upd[hr, hv])
        o_ref[rows, :] = jnp.concatenate(outs, axis=1)
        s_ref[...] = dec * s_old + jnp.concatenate(news, axis=0)

    @pl.when(j == pl.num_programs(1) - 1)
    def _():
        sout_ref[0] = s_ref[...]


def _gla_prompt(gq, gk, gv, misc2d, w2, ba, n, s, tc):
    nj = s // tc
    mall, lmask, hmask = _gla_consts()
    row = lambda w: pl.BlockSpec((tc, w), lambda b, j: (b * nj + j, 0))
    return pl.pallas_call(
        functools.partial(_gla_kernel, nchunk=tc // GLA_CHUNK),
        grid=(n, nj),
        in_specs=[row(GLA_KW), row(GLA_KW), row(GLA_WIDTH), row(MISC_W),
                  _const_spec(w2.shape), _const_spec(ba.shape), _const_spec(mall.shape),
                  _const_spec(lmask.shape), _const_spec(hmask.shape)],
        out_specs=[row(GLA_WIDTH), pl.BlockSpec((1, GLA_KW, GLA_DV), lambda b, j: (b, 0, 0))],
        out_shape=[jax.ShapeDtypeStruct((n * s, GLA_WIDTH), F32),
                   jax.ShapeDtypeStruct((n, GLA_KW, GLA_DV), F32)],
        scratch_shapes=[pltpu.VMEM((GLA_KW, GLA_DV), F32)],
        compiler_params=_cparams(("parallel", "arbitrary")),
        name="gla_prompt",
    )(gq, gk, gv, misc2d, w2, ba, mall, lmask, hmask)


def _softmax_rows(s_parts, s_new):
    m = s_new
    for s in s_parts:
        m = jnp.maximum(m, jnp.max(s, axis=1, keepdims=True))
    p_parts = [jnp.exp(s - m) for s in s_parts]
    p_new = jnp.exp(s_new - m)
    l = p_new
    for p in p_parts:
        l = l + jnp.sum(p, axis=1, keepdims=True)
    return p_parts, p_new, 1.0 / jnp.maximum(l, 1e-30)


def _bf16_round(x):
    return x.astype(BF16).astype(F32)


_HALF_ROWS = 2 * NSA_KVW


def _page_copies(cache_ref, pt_ref, bufs, sem, seq, npages):
    cbuf, sbuf = bufs
    copies = []
    for p in range(npages):
        pg = pt_ref[seq * npages + p]
        copies.append(pltpu.make_async_copy(cache_ref.at[pg, pl.ds(0, _HALF_ROWS), :],
                                            cbuf.at[:, p, :], sem))
        copies.append(pltpu.make_async_copy(cache_ref.at[pg, pl.ds(_HALF_ROWS, _HALF_ROWS), :],
                                            sbuf.at[pl.ds(p * _HALF_ROWS, _HALF_ROWS), :], sem))
    return copies


def _sample_consts(w_cmp, pe_cmp, npages, page):
    bpp = page // BLOCK
    z = jnp.zeros((BLOCK, NSA_HD, NSA_HD), w_cmp.dtype)
    per_c = []
    for c in range(2):
        w = w_cmp[c].transpose(1, 0, 2)
        rows = [jnp.concatenate([w if b2 == b1 else z for b2 in range(bpp)], axis=2)
                for b1 in range(bpp)]
        per_c.append(jnp.concatenate(rows, axis=1))
    w2t = jnp.stack(per_c).astype(BF16)
    pet = jnp.stack([jnp.tile(pe_cmp[c].T, (1, bpp)) for c in range(2)])
    pet = pet.reshape(2, NSA_HD, 1, page)
    j = np.arange(bpp * npages)
    col = np.arange(npages * page)
    expm = (j[:, None] == ((col % page) // BLOCK) * npages + col // page)
    return w2t, pet, jnp.asarray(expm.astype(np.float32), BF16)


def _sample_compute(bufs, j, qc_ref, qbd_ref, gts_ref, kvn_ref, wkvn_ref, wcol_ref, win_ref,
                    w2t_ref, pet_ref, exp_ref, o_ref, wout_ref, *, npages, page):
    cbuf, sbuf = bufs
    bpp = page // BLOCK
    nb = npages * bpp
    pos = npages * page
    cur = pos // BLOCK
    nh = NSA_HEADS
    dims_t = (((1,), (1,)), ((), ()))
    lane = lax.broadcasted_iota(jnp.int32, (nh, NSA_KVW), 1)
    qf = _bf16_round(qbd_ref[j] * (NSA_HD ** -0.5))
    qb = qf.astype(BF16)
    kvn = kvn_ref[j]
    wkvn = wkvn_ref[j]

    cmp_ = []
    for c in range(2):
        acc = jnp.zeros((NSA_KV_HEADS * npages, page), F32)
        for d in range(NSA_HD):
            x = jnp.concatenate([cbuf[c * NSA_KVW + g * NSA_HD + d]
                                 for g in range(NSA_KV_HEADS)], axis=0)
            acc += jnp.dot((x + pet_ref[c, d]).astype(BF16), w2t_ref[c, d],
                           preferred_element_type=F32)
        cmp_.append(jnp.concatenate([acc[g * npages:(g + 1) * npages]
                                     for g in range(NSA_KV_HEADS)], axis=1).astype(BF16))
    ck, cv = cmp_

    pidx = lax.broadcasted_iota(jnp.int32, (nh, npages), 1)
    sc, valid = [], []
    for bb in range(bpp):
        s = lax.dot_general((qc_ref[j, bb] * (NSA_HD ** -0.5)).astype(BF16), ck, dims_t,
                            preferred_element_type=F32)
        v = (bpp * pidx + bb + 1) * BLOCK - 1 <= pos
        sc.append(jnp.where(v, s, NEG))
        valid.append(v)
    mc = functools.reduce(jnp.maximum, [jnp.max(s, axis=1, keepdims=True) for s in sc])
    pc = [jnp.where(v, jnp.exp(s - mc), 0.0) for s, v in zip(sc, valid)]
    lc = functools.reduce(jnp.add, [jnp.sum(p, axis=1, keepdims=True) for p in pc])
    inv_c = 1.0 / jnp.maximum(lc, 1e-30)
    pc = [p * inv_c for p in pc]
    halves = []
    for g in range(NSA_KV_HEADS):
        t = None
        for bb in range(bpp):
            r = jnp.dot(pc[bb].astype(BF16), cv[:, g * NSA_KVW:(g + 1) * NSA_KVW],
                        preferred_element_type=F32)
            r = jnp.where((lane >= bb * BLOCK) & (lane < (bb + 1) * BLOCK), r, 0.0)
            t = r if t is None else t + r
        halves.append(t + pltpu.roll(t, BLOCK, axis=1))
    oc = jnp.where(lane < NSA_HD, halves[0], halves[1])

    width = -(-(nb + 1) // 128) * 128
    imp = [jnp.concatenate([jnp.broadcast_to(
        jnp.sum(p[g * NSA_HPG:(g + 1) * NSA_HPG], axis=0, keepdims=True), (NSA_HPG, npages))
        for g in range(NSA_KV_HEADS)], axis=0) for p in pc]
    imp = jnp.concatenate(imp + [jnp.zeros((nh, width - nb), F32)], axis=1)
    li = lax.broadcasted_iota(jnp.int32, (nh, width), 1)
    assert npages & (npages - 1) == 0
    shift = npages.bit_length() - 1
    bi = jnp.where(li < nb, bpp * jnp.bitwise_and(li, npages - 1)
                   + lax.shift_right_logical(li, shift), li)
    bf = bi.astype(F32)
    causal = bi <= cur
    forced = jnp.logical_and(causal, jnp.logical_or(bi == 0, bi > cur - N_LOCAL))
    score = jnp.where(forced, -4.0, jnp.where(causal, imp, -1.0))
    score = jnp.where(li <= nb, score, -3.0)

    if nb + 1 <= N_SELECT:
        sel = jnp.where(li <= nb, 1.0, 0.0)
    else:
        cols = jnp.concatenate([score, bf, jnp.zeros((128 - 2 * nh, width), F32)], axis=0).T
        earlier = jnp.where(cols[:, nh:nh + 1] < bf[0:1], 1.0, 0.0)
        sels = []
        for g in range(NSA_KV_HEADS):
            scol = cols[:, g * NSA_HPG:g * NSA_HPG + 1]
            srow = score[g * NSA_HPG:g * NSA_HPG + 1]
            ahead = jnp.where(scol > srow, 1.0, jnp.where(scol == srow, earlier, 0.0))
            rank = jnp.sum(ahead, axis=0, keepdims=True)
            take = jnp.logical_and(rank < N_SELECT - 1 - N_LOCAL, srow > -2.0)
            sels.append(jnp.broadcast_to(jnp.where(take, 1.0, 0.0), (NSA_HPG, width)))
        sel = jnp.where(forced, 1.0, jnp.concatenate(sels, axis=0))
    selb = jnp.where(sel > 0.0, 0.0, NEG)
    bias = jnp.dot(selb[:, :nb].astype(BF16), exp_ref[...], preferred_element_type=F32)
    selb_new = selb[:, nb:nb + 1]

    s_parts = []
    for p in range(npages):
        kt = sbuf[p * _HALF_ROWS:p * _HALF_ROWS + NSA_KVW, :].astype(BF16)
        s_parts.append(jnp.dot(qb, kt, preferred_element_type=F32)
                       + bias[:, p * page:(p + 1) * page])
    s_new = jnp.sum(qf * _bf16_round(kvn[:, 2 * NSA_KVW:3 * NSA_KVW]), axis=1,
                    keepdims=True) + selb_new
    p_parts, p_new, inv = _softmax_rows(s_parts, s_new)
    acc = _bf16_round(p_new) * _bf16_round(kvn[:, 3 * NSA_KVW:])
    for p in range(npages):
        vt = sbuf[p * _HALF_ROWS + NSA_KVW:(p + 1) * _HALF_ROWS, :].astype(BF16)
        acc += lax.dot_general(p_parts[p].astype(BF16), vt, dims_t, preferred_element_type=F32)
    os_ = acc * inv

    win = win_ref[j]
    wlen = win.shape[1]
    sw = jnp.dot(qb, win[:NSA_KVW].astype(BF16), preferred_element_type=F32)
    wpos = pos - wlen + lax.broadcasted_iota(jnp.int32, (nh, wlen), 1)
    dist = pos - wpos
    sw = jnp.where(jnp.logical_and(jnp.logical_and(dist >= 0, dist <= WINDOW), wpos >= 0),
                   sw, NEG)
    sw_new = jnp.sum(qf * _bf16_round(wkvn[:, :NSA_KVW]), axis=1, keepdims=True)
    (pw,), pw_new, inv_w = _softmax_rows([sw], sw_new)
    ow = (lax.dot_general(pw.astype(BF16), win[NSA_KVW:].astype(BF16), dims_t,
                          preferred_element_type=F32)
          + _bf16_round(pw_new) * _bf16_round(wkvn[:, NSA_KVW:])) * inv_w

    gt = _sigmoid(gts_ref[j])
    o_ref[j] = gt[:, 0:1] * oc + gt[:, 1:2] * os_ + gt[:, 2:3] * ow

    wl = lax.broadcasted_iota(jnp.int32, win.shape, 1)
    wout_ref[j] = jnp.where(wl == wlen - 1, wcol_ref[j], pltpu.roll(win, wlen - 1, axis=1))


def _nsa_sample_t_kernel(pt_ref, cache_ref, qc_ref, qbd_ref, gts_ref, kvn_ref, wkvn_ref, wcol_ref,
                         win_ref, w2t_ref, pet_ref, exp_ref, o_ref, wout_ref, cbuf0, sbuf0, cbuf1,
                         sbuf1, sem, *, npages, page):
    s = pl.program_id(0)
    buf0 = (cbuf0, sbuf0)
    buf1 = (cbuf1, sbuf1)
    compute = functools.partial(_sample_compute, qc_ref=qc_ref, qbd_ref=qbd_ref, gts_ref=gts_ref,
                                kvn_ref=kvn_ref, wkvn_ref=wkvn_ref, wcol_ref=wcol_ref,
                                win_ref=win_ref, w2t_ref=w2t_ref, pet_ref=pet_ref, exp_ref=exp_ref,
                                o_ref=o_ref, wout_ref=wout_ref, npages=npages, page=page)
    copies = lambda seq, buf, slot: _page_copies(cache_ref, pt_ref, buf, sem.at[slot], seq, npages)

    @pl.when(s == 0)
    def _():
        for c in copies(0, buf0, 0):
            c.start()

    for c in copies(2 * s + 1, buf1, 1):
        c.start()
    for c in copies(2 * s, buf0, 0):
        c.wait()
    compute(buf0, 0)

    @pl.when(s + 1 < pl.num_programs(0))
    def _():
        for c in copies(2 * s + 2, buf0, 0):
            c.start()

    for c in copies(2 * s + 1, buf1, 1):
        c.wait()
    compute(buf1, 1)


def _nsa_sample_t(page_table, cache_t, qc, qbd, gts3, kvn3, wkvn3, wcol, win_t, w2t, pet, expm):
    nd, npages = page_table.shape
    page = cache_t.shape[2]
    wlen = win_t.shape[2]
    assert nd % 2 == 0 and page % BLOCK == 0
    two = lambda *tail: pl.BlockSpec((2,) + tail, lambda s, pt: (s,) + (0,) * len(tail))
    const = lambda a: pl.BlockSpec(a.shape, lambda s, pt: (0,) * a.ndim,
                                   pipeline_mode=pl.Buffered(1))
    grid_spec = pltpu.PrefetchScalarGridSpec(
        num_scalar_prefetch=1,
        grid=(nd // 2,),
        in_specs=[pl.BlockSpec(memory_space=pl.ANY),
                  two(page // BLOCK, NSA_HEADS, 2 * NSA_KVW), two(NSA_HEADS, NSA_KVW),
                  two(NSA_HEADS, NSA_KVW), two(1, 4 * NSA_KVW), two(1, 2 * NSA_KVW),
                  two(2 * NSA_KVW, 1), two(2 * NSA_KVW, wlen), const(w2t), const(pet), const(expm)],
        out_specs=[two(NSA_HEADS, NSA_KVW), two(2 * NSA_KVW, wlen)],
        scratch_shapes=[pltpu.VMEM((_HALF_ROWS, npages, page), F32),
                        pltpu.VMEM((npages * _HALF_ROWS, page), F32)] * 2
                       + [pltpu.SemaphoreType.DMA((2,))])
    return pl.pallas_call(
        functools.partial(_nsa_sample_t_kernel, npages=npages, page=page),
        grid_spec=grid_spec,
        out_shape=[jax.ShapeDtypeStruct((nd, NSA_HEADS, NSA_KVW), F32),
                   jax.ShapeDtypeStruct(win_t.shape, F32)],
        compiler_params=_cparams(("arbitrary",)),
        name="nsa_sample",
    )(page_table.reshape(-1), cache_t, qc, qbd, gts3, kvn3, wkvn3, wcol, win_t, w2t, pet, expm)


def _gla_sample_kernel(gq_ref, gk_ref, gv_ref, misc_ref, w2_ref, ba_ref, s_ref, o_ref, sout_ref,
                       *, ns):
    la = _log_decay(misc_ref[...], w2_ref, ba_ref)
    pad = jnp.zeros((128 - 3 * ns, GLA_KW), F32)
    zt = jnp.concatenate([gq_ref[...] * (GLA_DK ** -0.5), gk_ref[...], jnp.exp(la), pad],
                         axis=0).T
    v = gv_ref[...]
    for i in range(ns):
        outs = []
        for h in range(GLA_HEADS):
            hr = slice(h * GLA_DK, (h + 1) * GLA_DK)
            qc = zt[hr, i:i + 1]
            kc = zt[hr, ns + i:ns + i + 1]
            ac = zt[hr, 2 * ns + i:2 * ns + i + 1]
            s_new = ac * s_ref[i, hr, :] + kc * v[i:i + 1, h * GLA_DV:(h + 1) * GLA_DV]
            sout_ref[i, hr, :] = s_new
            outs.append(jnp.sum(qc * s_new, axis=0, keepdims=True))
        o_ref[i:i + 1, :] = jnp.concatenate(outs, axis=1)


def _gla_sample(gq, gk, gv, misc2d, w2, ba, state3, ns=32):
    nd = gq.shape[0]
    row = lambda w: pl.BlockSpec((ns, w), lambda i: (i, 0))
    st = pl.BlockSpec((ns, GLA_KW, GLA_DV), lambda i: (i, 0, 0))
    return pl.pallas_call(
        functools.partial(_gla_sample_kernel, ns=ns),
        grid=(nd // ns,),
        in_specs=[row(GLA_KW), row(GLA_KW), row(GLA_WIDTH), row(MISC_W),
                  _const_spec(w2.shape), _const_spec(ba.shape), st],
        out_specs=[row(GLA_WIDTH), st],
        out_shape=[jax.ShapeDtypeStruct((nd, GLA_WIDTH), F32),
                   jax.ShapeDtypeStruct(state3.shape, F32)],
        compiler_params=_cparams(("parallel",)),
        name="gla_sample",
    )(gq, gk, gv, misc2d, w2, ba, state3)


def kernel(x_prompt, x_sample, cache_kv, cache_win, state_gla, page_table, p_prompt, p_sample,
           g_attn, w_in, w_cmp, pe_cmp, g_nsa_out, w_gla_a2, b_gla_a, g_gla_out, w_out, g_ffn,
           w_gate_up, w_down, g_ple, w_ple_gate, w_ple_proj, g_final):
    assert g_attn.shape[0] == 1, "single-layer trunk"
    n, s, d = x_prompt.shape
    nd, ds, _ = x_sample.shape
    assert ds == 1 and s % Q_BLOCK == 0 and s >= WINDOW
    w1 = _reorder_w_in(w_in[0])
    fw = _ffn_weights(g_nsa_out[0], g_gla_out[0], w_out[0], g_ffn[0], w_gate_up[0], w_down[0],
                      g_ple[0], w_ple_gate[0], w_ple_proj[0], g_final)
    cw2, cpe = _cmp_weights(w_cmp[0], pe_cmp[0])
    gw2, gba = _gla_gate_weights(w_gla_a2[0], b_gla_a[0])

    xp = x_prompt.reshape(n * s, d)
    q, kv, wkv, gq, gk, gv, gr, misc, kvt, wkvt = _project(xp, g_attn[0], w1, n, min(512, s))
    cmat, ksl, vst, kw, vwt = _nsa_prep(kv, wkv, kvt, wkvt, cw2, cpe, n, s, min(4096, s))
    o_n = _nsa_prompt(q, misc, cmat, ksl, vst, kw, vwt, n, s)
    o_g, s_p = _gla_prompt(gq, gk, gv, misc, gw2, gba, n, s, min(512, s))
    y_p = _mixer_ffn(o_n, o_g, gr, xp, p_prompt[0].reshape(n * s, -1), fw, min(512, n * s))

    xs = x_sample.reshape(nd, d)
    qs, kvs, wkvs, gqs, gks, gvs, grs, miscs, kvts, _ = _project(xs, g_attn[0], w1, 1, nd)
    page = cache_kv.shape[2]
    npages = page_table.shape[1]
    cache_t = _to_channel_major(cache_kv[0])
    win_t = _to_channel_major(cache_win[0])
    q4 = qs.reshape(nd, NSA_KV_HEADS, NSA_HPG, NSA_HD)
    qbd = jnp.concatenate([_place(q4[:, g], g, NSA_KV_HEADS) for g in range(NSA_KV_HEADS)],
                          axis=1)
    bpp = page // BLOCK
    qc = jnp.stack([jnp.concatenate([_place(q4[:, g], g * bpp + bb, NSA_KV_HEADS * bpp)
                                     for g in range(NSA_KV_HEADS)], axis=1)
                    for bb in range(bpp)], axis=1)
    gts3 = jnp.pad(miscs[:, :N_GATES].reshape(nd, NSA_HEADS, 3), ((0, 0), (0, 0), (0, NSA_KVW - 3)))
    w2t, pet, expm = _sample_consts(w_cmp[0], pe_cmp[0], npages, page)
    o8, win_new = _nsa_sample_t(page_table, cache_t, qc, qbd, gts3, kvs.reshape(nd, 1, -1),
                                wkvs.reshape(nd, 1, -1), wkvs.reshape(nd, -1, 1), win_t,
                                w2t, pet, expm)
    o8 = o8.reshape(nd, NSA_KV_HEADS, NSA_HPG, NSA_KV_HEADS, NSA_HD)
    o_ns = jnp.stack([o8[:, 0, :, 0], o8[:, 1, :, 1]], axis=1).reshape(nd, NSA_WIDTH)
    o_gs, s_s = _gla_sample(gqs, gks, gvs, miscs, gw2, gba,
                            state_gla[0].reshape(nd, GLA_KW, GLA_DV))
    y_s = _mixer_ffn(o_ns, o_gs, grs, xs, p_sample[0].reshape(nd, -1), fw, nd)

    wkeep = min(WINDOW, s)
    return (y_p.reshape(n, s, d),
            y_s.reshape(nd, 1, d),
            _from_channel_major(kvt, 4)[None],
            _from_channel_major(wkvt[:, :, s - wkeep:], 2)[None],
            s_p.reshape(1, n, GLA_HEADS, GLA_DK, GLA_DV),
            _from_channel_major(kvts, 4).reshape(1, nd, 1, 4, NSA_KV_HEADS, NSA_HD),
            _from_channel_major(win_new, 2)[None],
            s_s.reshape(1, nd, GLA_HEADS, GLA_DK, GLA_DV))


def _to_channel_major(x):
    n, t = x.shape[:2]
    return jnp.transpose(x, (0, 2, 3, 4, 1)).reshape(n, -1, t)


def _from_channel_major(xt, c):
    n, _, t = xt.shape
    return jnp.transpose(xt.reshape(n, c, NSA_KV_HEADS, NSA_HD, t), (0, 4, 1, 2, 3))


def _place(x, slot, nslots):
    z = jnp.zeros_like(x)
    return jnp.concatenate([x if i == slot else z for i in range(nslots)], axis=-1)
```

```python
import functools

import numpy as np
import jax
import jax.numpy as jnp
from jax import lax
from jax.experimental import pallas as pl
from jax.experimental.pallas import tpu as pltpu

F32 = jnp.float32
BF16 = jnp.bfloat16

EPS = 1e-6
NEG = -1e30
LOG2E = 1.4426950408889634

NSA_HEADS = 8
NSA_KV_HEADS = 2
NSA_HPG = NSA_HEADS // NSA_KV_HEADS
NSA_HD = 64
NSA_WIDTH = NSA_HEADS * NSA_HD
NSA_KVW = NSA_KV_HEADS * NSA_HD
BLOCK = 64
N_SELECT = 16
N_LOCAL = 2
WINDOW = 512
Q_BLOCK = 128
FORCE_SCORE = float(NSA_HPG + 1)
GLA_HEADS = 4
GLA_DK = 64
GLA_DV = 128
GLA_WIDTH = GLA_HEADS * GLA_DV
GLA_KW = GLA_HEADS * GLA_DK
GLA_GATE_RANK = 16
GLA_GATE_TEMP = 16.0
GLA_CHUNK = 64
N_GATES = 3 * NSA_HEADS
MISC_W = 128

VMEM_LIMIT = 56 * 1024 * 1024


def _cparams(sem):
    return pltpu.CompilerParams(dimension_semantics=sem, vmem_limit_bytes=VMEM_LIMIT)


def _const_spec(shape):
    nd = len(shape)
    return pl.BlockSpec(shape, lambda *_: (0,) * nd, pipeline_mode=pl.Buffered(1))


def _sigmoid(x):
    return 1.0 / (1.0 + jnp.exp(-x))


def _rms(x, g):
    ms = jnp.mean(x * x, axis=-1, keepdims=True)
    return x * lax.rsqrt(ms + EPS) * g


def _split3(x):
    hi = x.astype(BF16)
    r = x - hi.astype(F32)
    mid = r.astype(BF16)
    lo = (r - mid.astype(F32)).astype(BF16)
    return hi, mid, lo


_P_Q, _P_KV, _P_WKV, _P_GQ, _P_GK, _P_GV, _P_GR, _P_MISC, _P_END = (
    0, 512, 1024, 1280, 1536, 1792, 2304, 2816, 2944)


def _reorder_w_in(w_in):
    offs = np.cumsum([0, NSA_WIDTH, 4 * NSA_KVW, 2 * NSA_KVW, N_GATES, GLA_KW, GLA_KW,
                      GLA_WIDTH, GLA_GATE_RANK, GLA_WIDTH])
    q, kv, wkv, gts, gq, gk, gv, ga, gr = [w_in[:, offs[i]:offs[i + 1]] for i in range(9)]
    pad = jnp.zeros((w_in.shape[0], MISC_W - N_GATES - GLA_GATE_RANK), w_in.dtype)
    return jnp.concatenate([q, kv, wkv, gq, gk, gv, gr, gts, ga, pad], axis=1).astype(BF16)


def _proj_kernel(x_ref, g_ref, w_ref, wt_ref, q_ref, kv_ref, wkv_ref, gq_ref, gk_ref, gv_ref,
                 gr_ref, misc_ref, kvt_ref, wkvt_ref):
    xn = _rms(x_ref[...], g_ref[...]).astype(BF16)
    for ref, a, b in ((q_ref, _P_Q, _P_KV), (kv_ref, _P_KV, _P_WKV), (wkv_ref, _P_WKV, _P_GQ),
                      (gq_ref, _P_GQ, _P_GK), (gk_ref, _P_GK, _P_GV), (gv_ref, _P_GV, _P_GR),
                      (gr_ref, _P_GR, _P_MISC), (misc_ref, _P_MISC, _P_END)):
        ref[...] = jnp.dot(xn, w_ref[:, a:b], preferred_element_type=F32)
    dims_t = (((1,), (1,)), ((), ()))
    nkv = kvt_ref.shape[1]
    kvt_ref[0] = lax.dot_general(wt_ref[0:nkv, :], xn, dims_t, preferred_element_type=F32)
    wkvt_ref[0] = lax.dot_general(wt_ref[nkv:, :], xn, dims_t, preferred_element_type=F32)


def _project(x2d, g_attn, w1, n, tm):
    t, d = x2d.shape
    s = t // n
    nj = s // tm
    widths = (512, 512, 256, 256, 256, 512, 512, MISC_W)
    wt = w1[:, _P_KV:_P_GQ].T
    tspec = lambda rows: pl.BlockSpec((1, rows, tm), lambda i: (i // nj, 0, i % nj))
    return pl.pallas_call(
        _proj_kernel,
        grid=(t // tm,),
        in_specs=[pl.BlockSpec((tm, d), lambda i: (i, 0)),
                  _const_spec((1, d)),
                  _const_spec(w1.shape),
                  _const_spec(wt.shape)],
        out_specs=[pl.BlockSpec((tm, w), lambda i: (i, 0)) for w in widths]
                  + [tspec(4 * NSA_KVW), tspec(2 * NSA_KVW)],
        out_shape=[jax.ShapeDtypeStruct((t, w), F32) for w in widths]
                  + [jax.ShapeDtypeStruct((n, 4 * NSA_KVW, s), F32),
                     jax.ShapeDtypeStruct((n, 2 * NSA_KVW, s), F32)],
        compiler_params=_cparams(("parallel",)),
        name="proj",
    )(x2d, g_attn.reshape(1, d), w1, wt)


def _group_mean_sq(x, bd_ref, width):
    sq = x * x
    hi = sq.astype(BF16)
    lo = (sq - hi.astype(F32)).astype(BF16)
    bd = bd_ref[...]
    s = jnp.dot(hi, bd, preferred_element_type=F32) + jnp.dot(lo, bd, preferred_element_type=F32)
    return s * (1.0 / width)


def _ffn_kernel(on_ref, og_ref, gr_ref, x_ref, p_ref, gn_ref, gg_ref, bd64_ref, bd128_ref,
                wo_ref, gffn_ref, wgu_ref, wd_ref, gple_ref, wpg_ref, wpp_ref, gfin_ref,
                y_ref, h_ref, acc_ref, *, n_ff):
    o_n = on_ref[...]
    o_g = og_ref[...]
    r = gr_ref[...]
    a_n = o_n * lax.rsqrt(_group_mean_sq(o_n, bd64_ref, NSA_HD) + EPS) * gn_ref[...]
    a_g = o_g * lax.rsqrt(_group_mean_sq(o_g, bd128_ref, GLA_DV) + EPS) * gg_ref[...]
    a_g = a_g * (r * _sigmoid(r))
    x1 = (x_ref[...]
          + jnp.dot(a_n.astype(BF16), wo_ref[0:NSA_WIDTH, :], preferred_element_type=F32)
          + jnp.dot(a_g.astype(BF16), wo_ref[NSA_WIDTH:, :], preferred_element_type=F32))
    h_ref[...] = _rms(x1, gffn_ref[...]).astype(BF16)
    acc_ref[...] = x1

    def body(c, carry):
        gu = jnp.dot(h_ref[...], wgu_ref[c], preferred_element_type=F32)
        half = gu.shape[1] // 2
        gt = gu[:, :half]
        act = (gt * _sigmoid(gt) * gu[:, half:]).astype(BF16)
        acc_ref[...] += jnp.dot(act, wd_ref[c], preferred_element_type=F32)
        return carry

    lax.fori_loop(0, n_ff, body, 0)
    x2 = acc_ref[...]
    gate = _sigmoid(jnp.dot(_rms(x2, gple_ref[...]).astype(BF16), wpg_ref[...],
                            preferred_element_type=F32))
    x3 = x2 + jnp.dot(p_ref[...].astype(BF16), wpp_ref[...], preferred_element_type=F32) * gate
    y_ref[...] = _rms(x3, gfin_ref[...])


def _block_diag_ones(n, width):
    idx = np.arange(n) // width
    return jnp.asarray((idx[:, None] == idx[None, :]).astype(np.float32), BF16)


def _ffn_weights(g_nsa_out, g_gla_out, w_out, g_ffn, w_gate_up, w_down, g_ple, w_ple_gate,
                 w_ple_proj, g_final, ff_chunk=256):
    d = w_out.shape[1]
    d_ff = w_down.shape[0]
    n_ff = d_ff // ff_chunk
    wg = w_gate_up[:, :d_ff].reshape(d, n_ff, ff_chunk)
    wu = w_gate_up[:, d_ff:].reshape(d, n_ff, ff_chunk)
    wgu = jnp.concatenate([wg, wu], axis=2).transpose(1, 0, 2).astype(BF16)
    wd = w_down.reshape(n_ff, ff_chunk, d).astype(BF16)
    return dict(gn=g_nsa_out.reshape(1, NSA_WIDTH), gg=g_gla_out.reshape(1, GLA_WIDTH),
                bd64=_block_diag_ones(NSA_WIDTH, NSA_HD), bd128=_block_diag_ones(GLA_WIDTH, GLA_DV),
                wo=w_out.astype(BF16), gffn=g_ffn.reshape(1, d), wgu=wgu, wd=wd,
                gple=g_ple.reshape(1, d), wpg=w_ple_gate.astype(BF16),
                wpp=w_ple_proj.astype(BF16), gfin=g_final.reshape(1, d))


def _mixer_ffn(o_n, o_g, gr, x2d, p2d, fw, tm):
    t, d = x2d.shape
    n_ff = fw["wgu"].shape[0]
    consts = [fw[k] for k in ("gn", "gg", "bd64", "bd128", "wo", "gffn", "wgu", "wd", "gple",
                              "wpg", "wpp", "gfin")]
    row = lambda w: pl.BlockSpec((tm, w), lambda i: (i, 0))
    return pl.pallas_call(
        functools.partial(_ffn_kernel, n_ff=n_ff),
        grid=(t // tm,),
        in_specs=[row(NSA_WIDTH), row(GLA_WIDTH), row(GLA_WIDTH), row(d), row(p2d.shape[1])]
                 + [_const_spec(c.shape) for c in consts],
        out_specs=row(d),
        out_shape=jax.ShapeDtypeStruct((t, d), F32),
        scratch_shapes=[pltpu.VMEM((tm, d), BF16), pltpu.VMEM((tm, d), F32)],
        compiler_params=_cparams(("parallel",)),
        name="mixer_ffn",
    )(o_n, o_g, gr, x2d, p2d, *consts)


def _cmp_weights(w_cmp, pe_cmp):
    z = jnp.zeros((BLOCK, NSA_HD, NSA_HD), w_cmp.dtype)

    def bd(w):
        return jnp.concatenate([jnp.concatenate([w, z], axis=2),
                                jnp.concatenate([z, w], axis=2)], axis=1)

    w2 = jnp.stack([bd(w_cmp[0]), bd(w_cmp[1])], axis=1).astype(BF16)
    pe = jnp.stack([jnp.concatenate([pe_cmp[0], pe_cmp[0]], axis=1),
                    jnp.concatenate([pe_cmp[1], pe_cmp[1]], axis=1)], axis=1)
    return w2, pe.reshape(BLOCK, 2, 1, NSA_KVW)


def _compress(xk_ref, xv_ref, w2_ref, pe_ref, nb):
    acc_k = jnp.zeros((nb, NSA_KVW), F32)
    acc_v = jnp.zeros((nb, NSA_KVW), F32)
    for l in range(BLOCK):
        xk = (xk_ref[pl.ds(l, nb, stride=BLOCK), :] + pe_ref[l, 0]).astype(BF16)
        xv = (xv_ref[pl.ds(l, nb, stride=BLOCK), :] + pe_ref[l, 1]).astype(BF16)
        acc_k += jnp.dot(xk, w2_ref[l, 0], preferred_element_type=F32)
        acc_v += jnp.dot(xv, w2_ref[l, 1], preferred_element_type=F32)
    return acc_k, acc_v


def _prep_kernel(ckin_ref, cvin_ref, ksin_ref, kwin_ref, vst_in_ref, vwt_in_ref, w2_ref, pe_ref,
                 c_ref, ks_ref, vst_ref, kw_ref, vwt_ref, *, tk):
    nb = tk // BLOCK
    ck, cv = _compress(ckin_ref, cvin_ref, w2_ref, pe_ref, nb)
    c_ref[:, 0:NSA_KVW] = ck
    c_ref[:, NSA_KVW:] = cv
    ks_ref[...] = ksin_ref[...].astype(BF16)
    kw_ref[...] = kwin_ref[...].astype(BF16)
    vst_ref[...] = vst_in_ref[...].astype(BF16)
    vwt_ref[...] = vwt_in_ref[...].astype(BF16)


def _nsa_prep(kv2d, wkv2d, kvt, wkvt, w2, pe, n, s, tk):
    t = n * s
    nj = s // tk
    kv_specs = [pl.BlockSpec((tk, NSA_KVW), (lambda b, j, c=c: (b * nj + j, c))) for c in range(3)]
    return pl.pallas_call(
        functools.partial(_prep_kernel, tk=tk),
        grid=(n, nj),
        in_specs=kv_specs + [pl.BlockSpec((tk, NSA_KVW), lambda b, j: (b * nj + j, 0)),
                             pl.BlockSpec((1, NSA_KVW, tk), lambda b, j: (b, 3, j)),
                             pl.BlockSpec((1, NSA_KVW, tk), lambda b, j: (b, 1, j)),
                             _const_spec(w2.shape), _const_spec(pe.shape)],
        out_specs=[pl.BlockSpec((tk // BLOCK, 2 * NSA_KVW), lambda b, j: (b * nj + j, 0)),
                   pl.BlockSpec((tk, NSA_KVW), lambda b, j: (b * nj + j, 0)),
                   pl.BlockSpec((1, NSA_KVW, tk), lambda b, j: (b, 0, j)),
                   pl.BlockSpec((tk, NSA_KVW), lambda b, j: (b * nj + j, 0)),
                   pl.BlockSpec((1, NSA_KVW, tk), lambda b, j: (b, 0, j))],
        out_shape=[jax.ShapeDtypeStruct((t // BLOCK, 2 * NSA_KVW), F32),
                   jax.ShapeDtypeStruct((t, NSA_KVW), BF16),
                   jax.ShapeDtypeStruct((n, NSA_KVW, s), BF16),
                   jax.ShapeDtypeStruct((t, NSA_KVW), BF16),
                   jax.ShapeDtypeStruct((n, NSA_KVW, s), BF16)],
        compiler_params=_cparams(("parallel", "parallel")),
        name="nsa_prep",
    )(kv2d, kv2d, kv2d, wkv2d, kvt, wkvt, w2, pe)


_LQ = NSA_KV_HEADS * NSA_HPG * Q_BLOCK
_LG = NSA_HPG * Q_BLOCK
_SEL_SHIFT = 2
_SEL_GROUP = Q_BLOCK << _SEL_SHIFT


def _rep_heads(x):
    a, b = x[:, :Q_BLOCK], x[:, Q_BLOCK:]
    return jnp.concatenate([a] * NSA_HPG + [b] * NSA_HPG, axis=1)


def _flash_update(s, vt, carry):
    m, l, acc0, acc1 = carry
    m_new = jnp.maximum(m, jnp.max(s, axis=0, keepdims=True))
    alpha = jnp.exp2(m - m_new)
    p = jnp.exp2(s - m_new)
    l = alpha * l + jnp.sum(p, axis=0, keepdims=True)
    pb = p.astype(BF16)
    acc0 = alpha[:, :_LG] * acc0 + jnp.dot(vt[:NSA_HD], pb[:, :_LG], preferred_element_type=F32)
    acc1 = alpha[:, _LG:] * acc1 + jnp.dot(vt[NSA_HD:], pb[:, _LG:], preferred_element_type=F32)
    return m_new, l, acc0, acc1


def _flash_init():
    return (jnp.full((1, _LQ), NEG, F32), jnp.zeros((1, _LQ), F32),
            jnp.zeros((NSA_HD, _LG), F32), jnp.zeros((NSA_HD, _LG), F32))


def _flash_out(carry):
    m, l, acc0, acc1 = carry
    inv = 1.0 / jnp.maximum(l, 1e-30)
    return acc0 * inv[:, :_LG], acc1 * inv[:, _LG:]


def _select_blocks(imps, cur, nbs):
    if nbs <= N_SELECT:
        return [jnp.ones(imp.shape, F32) for imp in imps]
    bi = lax.broadcasted_iota(jnp.int32, imps[0].shape, 0)
    bf = bi.astype(F32)
    causal = bi <= cur
    forced = jnp.logical_and(causal, jnp.logical_or(bi == 0, bi > cur - N_LOCAL))
    scores = tuple(jnp.where(forced, -2.0, jnp.where(causal, imp, -1.0)) for imp in imps)

    def take_one(score):
        mx = jnp.max(score, axis=0, keepdims=True)
        idx = jnp.min(jnp.where(score == mx, bf, float(nbs)), axis=0, keepdims=True)
        return jnp.where(bf == idx, -2.0, score)

    scores = lax.fori_loop(0, N_SELECT - 1 - N_LOCAL,
                           lambda _, sc: tuple(take_one(s) for s in sc), scores)
    return [jnp.where(s == -2.0, 1.0, 0.0) for s in scores]


def _nsa_prompt_kernel(q_ref, misc_ref, c_ref, ks_ref, vst_ref, ind_ref, indw_ref, *rest, nbc):
    kw_refs = rest[0:5]
    vwt_refs = rest[5:10]
    o_ref = rest[10]
    selb_ref, s0_ref, s1_ref, p0_ref, p1_ref = rest[11:16]
    i = pl.program_id(1)

    qT = (q_ref[...] * (NSA_HD ** -0.5 * LOG2E)).T
    z = jnp.zeros((NSA_HD, Q_BLOCK), F32)
    top = [qT[NSA_HD * h:NSA_HD * (h + 1)] for h in range(NSA_HPG)] + [z] * NSA_HPG
    bot = [z] * NSA_HPG + [qT[NSA_HD * h:NSA_HD * (h + 1)] for h in range(NSA_HPG, NSA_HEADS)]
    qt = jnp.concatenate([jnp.concatenate(top, axis=1), jnp.concatenate(bot, axis=1)],
                         axis=0).astype(BF16)

    cmat = c_ref[...]
    ck = cmat[:, :NSA_KVW].astype(BF16)
    cv = cmat[:, NSA_KVW:].astype(BF16)
    sc = jnp.dot(ck, qt, preferred_element_type=F32)
    b_io = lax.broadcasted_iota(jnp.int32, (nbc, _LQ), 0)
    qoff = jnp.bitwise_and(lax.broadcasted_iota(jnp.int32, (nbc, _LQ), 1), Q_BLOCK - 1)
    valid = b_io * BLOCK + (BLOCK - 1) <= i * Q_BLOCK + qoff
    sc = jnp.where(valid, sc, NEG)
    mc = jnp.max(sc, axis=0, keepdims=True)
    pc = jnp.where(valid, jnp.exp2(sc - mc), 0.0)
    pc = pc / jnp.maximum(jnp.sum(pc, axis=0, keepdims=True), 1e-30)
    oc_full = lax.dot_general(cv, pc.astype(BF16), (((0,), (0,)), ((), ())),
                              preferred_element_type=F32)
    oc = (oc_full[:NSA_HD, :_LG], oc_full[NSA_HD:, _LG:])

    qo = lax.broadcasted_iota(jnp.int32, (1, Q_BLOCK), 1)
    cur = 2 * i + (qo >= BLOCK).astype(jnp.int32)
    imps = []
    for g in range(NSA_KV_HEADS):
        imp = pc[:, g * _LG:g * _LG + Q_BLOCK]
        for hh in range(1, NSA_HPG):
            imp = imp + pc[:, g * _LG + hh * Q_BLOCK:g * _LG + (hh + 1) * Q_BLOCK]
        imps.append(imp)
    for g, sel in enumerate(_select_blocks(imps, cur, nbc)):
        selb_ref[:, g * Q_BLOCK:(g + 1) * Q_BLOCK] = jnp.where(sel > 0.0, 0.0, NEG)

    nbg = _SEL_GROUP // BLOCK
    ind = ind_ref[...]
    zpad = jnp.zeros((NSA_KVW - 16, _LQ), BF16)

    def scores(jg):
        off = pl.multiple_of(jg * _SEL_GROUP, _SEL_GROUP)
        ka = jnp.concatenate([ks_ref[pl.ds(off, _SEL_GROUP), :], ind], axis=1)
        brow = selb_ref[pl.ds(pl.multiple_of(jg * nbg, nbg), nbg), :]
        baug = jnp.concatenate([_rep_heads(brow), jnp.zeros((16 - nbg, _LQ), F32)],
                               axis=0).astype(BF16)
        return jnp.dot(ka, jnp.concatenate([qt, baug, zpad], axis=0), preferred_element_type=F32)

    def values(jg, p_ref):
        vt = vst_ref[0, :, pl.ds(pl.multiple_of(jg * _SEL_GROUP, _SEL_GROUP), _SEL_GROUP)]
        return (jnp.dot(vt[:NSA_HD], p_ref[:, :_LG], preferred_element_type=F32),
                jnp.dot(vt[NSA_HD:], p_ref[:, _LG:], preferred_element_type=F32))

    def stage(jg, s_cur, p_cur, s_nxt, p_prev, st, diagonal=False):
        m, l, b0, b1 = st
        if s_nxt is not None:
            s_nxt[...] = scores(jg + 1)
        pv0, pv1 = values(jnp.maximum(jg - 1, 0), p_prev)
        s = s_cur[...]
        if diagonal:
            rowg = lax.broadcasted_iota(jnp.int32, (_SEL_GROUP, _LQ), 0)
            qcg = jnp.bitwise_and(lax.broadcasted_iota(jnp.int32, (_SEL_GROUP, _LQ), 1),
                                  Q_BLOCK - 1)
            s = jnp.where(jg * _SEL_GROUP + rowg <= i * Q_BLOCK + qcg, s, NEG)
        m_new = jnp.maximum(m, jnp.max(s, axis=0, keepdims=True))
        alpha = jnp.exp2(m - m_new)
        p = jnp.exp2(s - m_new)
        p_cur[...] = p.astype(BF16)
        l = alpha * l + jnp.sum(p, axis=0, keepdims=True)
        return m_new, l, alpha[:, :_LG] * (b0 + pv0), alpha[:, _LG:] * (b1 + pv1)

    def finish(st, p_last, jg):
        m, l, b0, b1 = st
        pv0, pv1 = values(jg, p_last)
        inv = 1.0 / jnp.maximum(l, 1e-30)
        return (b0 + pv0) * inv[:, :_LG], (b1 + pv1) * inv[:, _LG:]

    nfull = lax.shift_right_logical(i, _SEL_SHIFT)
    s0_ref[...] = scores(0)
    p1_ref[...] = jnp.zeros(p1_ref.shape, BF16)

    def pair(t, st):
        st = stage(2 * t, s0_ref, p0_ref, s1_ref, p1_ref, st)
        return stage(2 * t + 1, s1_ref, p1_ref, s0_ref, p0_ref, st)

    st = lax.fori_loop(0, lax.shift_right_logical(nfull, 1), pair, _flash_init())

    def odd_tail(st):
        st = stage(nfull - 1, s0_ref, p0_ref, s1_ref, p1_ref, st)
        st = stage(nfull, s1_ref, p1_ref, None, p0_ref, st, diagonal=True)
        return finish(st, p1_ref, nfull)

    def even_tail(st):
        st = stage(nfull, s0_ref, p0_ref, None, p1_ref, st, diagonal=True)
        return finish(st, p0_ref, nfull)

    os_ = lax.cond(jnp.bitwise_and(nfull, 1) == 1, odd_tail, even_tail, st)

    nwt = len(kw_refs)
    row = lax.broadcasted_iota(jnp.int32, (Q_BLOCK, _LQ), 0)
    qcol = jnp.bitwise_and(lax.broadcasted_iota(jnp.int32, (Q_BLOCK, _LQ), 1), Q_BLOCK - 1)
    tile = lax.broadcasted_iota(jnp.int32, (16, _LQ), 0)
    wbias = jnp.where(tile < nwt - 1 - i, NEG, 0.0).astype(BF16)
    kwa = jnp.concatenate([jnp.concatenate([r[...] for r in kw_refs], axis=0), indw_ref[...]],
                          axis=1)
    sw = jnp.dot(kwa, jnp.concatenate([qt, wbias, zpad], axis=0), preferred_element_type=F32)
    sw = jnp.concatenate([jnp.where(row >= qcol, sw[:Q_BLOCK], NEG),
                          sw[Q_BLOCK:(nwt - 1) * Q_BLOCK],
                          jnp.where(row <= qcol, sw[(nwt - 1) * Q_BLOCK:], NEG)], axis=0)
    ow = _flash_out(_flash_update(sw, jnp.concatenate([r[0] for r in vwt_refs], axis=1),
                                  _flash_init()))

    gt = _sigmoid(misc_ref[...]).T
    outs = []
    for g in range(NSA_KV_HEADS):
        for hh in range(NSA_HPG):
            h = g * NSA_HPG + hh
            sl = slice(hh * Q_BLOCK, (hh + 1) * Q_BLOCK)
            outs.append(gt[3 * h:3 * h + 1] * oc[g][:, sl] + gt[3 * h + 1:3 * h + 2] * os_[g][:, sl]
                        + gt[3 * h + 2:3 * h + 3] * ow[g][:, sl])
    o_ref[...] = jnp.concatenate(outs, axis=0).T


def _nsa_prompt(q2d, misc2d, cmat, ks, vst, kw, vwt, n, s):
    assert s % _SEL_GROUP == 0
    nq = s // Q_BLOCK
    nbc = s // BLOCK
    blk = np.arange(_SEL_GROUP) // BLOCK
    ind = jnp.asarray((blk[:, None] == np.arange(NSA_KVW)[None, :]).astype(np.float32), BF16)
    wtile = np.arange(5 * Q_BLOCK) // Q_BLOCK
    indw = jnp.asarray((wtile[:, None] == np.arange(NSA_KVW)[None, :]).astype(np.float32), BF16)
    wk_specs = [pl.BlockSpec((Q_BLOCK, NSA_KVW),
                             (lambda b, i, t=t: (b * nq + jnp.maximum(i - 4 + t, 0), 0)))
                for t in range(5)]
    wv_specs = [pl.BlockSpec((1, NSA_KVW, Q_BLOCK),
                             (lambda b, i, t=t: (b, 0, jnp.maximum(i - 4 + t, 0))))
                for t in range(5)]
    return pl.pallas_call(
        functools.partial(_nsa_prompt_kernel, nbc=nbc),
        grid=(n, nq),
        in_specs=[pl.BlockSpec((Q_BLOCK, NSA_WIDTH), lambda b, i: (b * nq + i, 0)),
                  pl.BlockSpec((Q_BLOCK, MISC_W), lambda b, i: (b * nq + i, 0)),
                  pl.BlockSpec((nbc, 2 * NSA_KVW), lambda b, i: (b, 0)),
                  pl.BlockSpec((s, NSA_KVW), lambda b, i: (b, 0)),
                  pl.BlockSpec((1, NSA_KVW, s), lambda b, i: (b, 0, 0)),
                  _const_spec(ind.shape), _const_spec(indw.shape)] + wk_specs + wv_specs,
        out_specs=pl.BlockSpec((Q_BLOCK, NSA_WIDTH), lambda b, i: (b * nq + i, 0)),
        out_shape=jax.ShapeDtypeStruct((n * s, NSA_WIDTH), F32),
        scratch_shapes=[pltpu.VMEM((nbc, 2 * Q_BLOCK), F32),
                        pltpu.VMEM((_SEL_GROUP, _LQ), F32), pltpu.VMEM((_SEL_GROUP, _LQ), F32),
                        pltpu.VMEM((_SEL_GROUP, _LQ), BF16), pltpu.VMEM((_SEL_GROUP, _LQ), BF16)],
        compiler_params=_cparams(("parallel", "arbitrary")),
        name="nsa_prompt",
    )(q2d, misc2d, cmat, ks, vst, ind, indw, *([kw] * 5), *([vwt] * 5))


_N_LEVELS = int(np.log2(GLA_CHUNK))


def _gla_consts():
    c = GLA_CHUNK
    t = np.arange(c)
    mats = [t[None, :] <= t[:, None], t[None, :] > t[:, None]]
    masks = []
    w = c // 2
    while w >= 1:
        blk = t // (2 * w)
        mid = blk * 2 * w + w
        upper = t >= mid
        m = np.zeros((c, c), bool)
        for r in range(c):
            if upper[r]:
                m[r, mid[r]:r + 1] = True
            else:
                m[r, r + 1:mid[r]] = True
        mats.append(m)
        masks.append((blk[:, None] == blk[None, :]) & upper[:, None] & ~upper[None, :])
        w //= 2
    masks.append(np.eye(c, dtype=bool))
    mall = jnp.asarray(np.concatenate(mats, axis=0).astype(np.float32), BF16)
    lmask = jnp.asarray(np.tile(np.stack(masks).astype(np.float32), (1, GLA_HEADS, 1)))
    hd = np.arange(GLA_KW) // GLA_DK
    hmask = jnp.asarray((hd[:, None] == hd[None, :]).astype(np.float32))
    return mall, lmask, hmask


def _gla_gate_weights(w_a2, b_a):
    w2 = jnp.zeros((MISC_W, GLA_KW), F32).at[N_GATES:N_GATES + GLA_GATE_RANK].set(w_a2)
    return w2.astype(BF16), b_a.reshape(1, GLA_KW)


def _log_decay(misc, w2_ref, ba_ref):
    x = jnp.dot(misc.astype(BF16), w2_ref[...], preferred_element_type=F32) + ba_ref[...]
    return (jnp.minimum(x, 0.0) - jnp.log1p(jnp.exp(-jnp.abs(x)))) * (1.0 / GLA_GATE_TEMP)


def _gla_kernel(gq_ref, gk_ref, gv_ref, misc_ref, w2_ref, ba_ref, mall_ref, lmask_ref, hm_ref,
                o_ref, sout_ref, s_ref, *, nchunk):
    j = pl.program_id(1)

    @pl.when(j == 0)
    def _():
        s_ref[...] = jnp.zeros(s_ref.shape, F32)

    la_all = _log_decay(misc_ref[...], w2_ref, ba_ref)
    mall = mall_ref[...]
    hm = hm_ref[...]
    c = GLA_CHUNK
    dims_t = (((1,), (1,)), ((), ()))

    def stack(x):
        return (jnp.concatenate([x] * GLA_HEADS, axis=0) * hm).astype(BF16)

    for ci in range(nchunk):
        rows = slice(ci * c, (ci + 1) * c)
        q = gq_ref[rows, :] * (GLA_DK ** -0.5)
        k = gk_ref[rows, :]
        v = gv_ref[rows, :].astype(BF16)
        la = la_all[rows]
        hi, mid, lo = _split3(la)
        ex = jnp.exp(jnp.dot(mall, hi, preferred_element_type=F32)
                     + jnp.dot(mall, mid, preferred_element_type=F32)
                     + jnp.dot(mall, lo, preferred_element_type=F32))
        att = lax.dot_general(stack(q), k.astype(BF16), dims_t,
                              preferred_element_type=F32) * lmask_ref[_N_LEVELS]
        for lev in range(_N_LEVELS):
            e = ex[(2 + lev) * c:(3 + lev) * c]
            att += lax.dot_general(stack(q * e), (k * e).astype(BF16), dims_t,
                                   preferred_element_type=F32) * lmask_ref[lev]
        attb = att.astype(BF16)
        s_old = s_ref[...]
        o_inter = jnp.dot(stack(q * ex[0:c]), s_old.astype(BF16), preferred_element_type=F32)
        kd = (k * ex[c:2 * c]).astype(BF16)
        upd = lax.dot_general(kd, v, (((0,), (0,)), ((), ())), preferred_element_type=F32)
        dec = jnp.exp(jnp.sum(la.T, axis=1, keepdims=True))
        outs, news = [], []
        for h in range(GLA_HEADS):
            hr = slice(h * GLA_DK, (h + 1) * GLA_DK)
            hv = slice(h * GLA_DV, (h + 1) * GLA_DV)
            outs.append(o_inter[hr] + jnp.dot(attb[hr], v[:, hv], preferred_element_type=F32))
            news.append(upd[hr, hv])
        o_ref[rows, :] = jnp.concatenate(outs, axis=1)
        s_ref[...] = dec * s_old + jnp.concatenate(news, axis=0)

    @pl.when(j == pl.num_programs(1) - 1)
    def _():
        sout_ref[0] = s_ref[...]


def _gla_prompt(gq, gk, gv, misc2d, w2, ba, n, s, tc):
    nj = s // tc
    mall, lmask, hmask = _gla_consts()
    row = lambda w: pl.BlockSpec((tc, w), lambda b, j: (b * nj + j, 0))
    return pl.pallas_call(
        functools.partial(_gla_kernel, nchunk=tc // GLA_CHUNK),
        grid=(n, nj),
        in_specs=[row(GLA_KW), row(GLA_KW), row(GLA_WIDTH), row(MISC_W),
                  _const_spec(w2.shape), _const_spec(ba.shape), _const_spec(mall.shape),
                  _const_spec(lmask.shape), _const_spec(hmask.shape)],
        out_specs=[row(GLA_WIDTH), pl.BlockSpec((1, GLA_KW, GLA_DV), lambda b, j: (b, 0, 0))],
        out_shape=[jax.ShapeDtypeStruct((n * s, GLA_WIDTH), F32),
                   jax.ShapeDtypeStruct((n, GLA_KW, GLA_DV), F32)],
        scratch_shapes=[pltpu.VMEM((GLA_KW, GLA_DV), F32)],
        compiler_params=_cparams(("parallel", "arbitrary")),
        name="gla_prompt",
    )(gq, gk, gv, misc2d, w2, ba, mall, lmask, hmask)


def _softmax_rows(s_parts, s_new):
    m = s_new
    for s in s_parts:
        m = jnp.maximum(m, jnp.max(s, axis=1, keepdims=True))
    p_parts = [jnp.exp(s - m) for s in s_parts]
    p_new = jnp.exp(s_new - m)
    l = p_new
    for p in p_parts:
        l = l + jnp.sum(p, axis=1, keepdims=True)
    return p_parts, p_new, 1.0 / jnp.maximum(l, 1e-30)


def _bf16_round(x):
    return x.astype(BF16).astype(F32)


_HALF_ROWS = 2 * NSA_KVW


def _page_copies(cache_ref, pt_ref, bufs, sem, seq, npages):
    cbuf, sbuf = bufs
    copies = []
    for p in range(npages):
        pg = pt_ref[seq * npages + p]
        copies.append(pltpu.make_async_copy(cache_ref.at[pg, pl.ds(0, _HALF_ROWS), :],
                                            cbuf.at[:, p, :], sem))
        copies.append(pltpu.make_async_copy(cache_ref.at[pg, pl.ds(_HALF_ROWS, _HALF_ROWS), :],
                                            sbuf.at[pl.ds(p * _HALF_ROWS, _HALF_ROWS), :], sem))
    return copies


def _sample_consts(w_cmp, pe_cmp, npages, page):
    bpp = page // BLOCK
    z = jnp.zeros((BLOCK, NSA_HD, NSA_HD), w_cmp.dtype)
    per_c = []
    for c in range(2):
        w = w_cmp[c].transpose(1, 0, 2)
        rows = [jnp.concatenate([w if b2 == b1 else z for b2 in range(bpp)], axis=2)
                for b1 in range(bpp)]
        per_c.append(jnp.concatenate(rows, axis=1))
    w2t = jnp.stack(per_c).astype(BF16)
    pet = jnp.stack([jnp.tile(pe_cmp[c].T, (1, bpp)) for c in range(2)])
    pet = pet.reshape(2, NSA_HD, 1, page)
    j = np.arange(bpp * npages)
    col = np.arange(npages * page)
    expm = (j[:, None] == ((col % page) // BLOCK) * npages + col // page)
    return w2t, pet, jnp.asarray(expm.astype(np.float32), BF16)


def _sample_compute(bufs, j, qc_ref, qbd_ref, gts_ref, kvn_ref, wkvn_ref, wcol_ref, win_ref,
                    w2t_ref, pet_ref, exp_ref, o_ref, wout_ref, *, npages, page):
    cbuf, sbuf = bufs
    bpp = page // BLOCK
    nb = npages * bpp
    pos = npages * page
    cur = pos // BLOCK
    nh = NSA_HEADS
    dims_t = (((1,), (1,)), ((), ()))
    lane = lax.broadcasted_iota(jnp.int32, (nh, NSA_KVW), 1)
    qf = _bf16_round(qbd_ref[j] * (NSA_HD ** -0.5))
    qb = qf.astype(BF16)
    kvn = kvn_ref[j]
    wkvn = wkvn_ref[j]

    cmp_ = []
    for c in range(2):
        acc = jnp.zeros((NSA_KV_HEADS * npages, page), F32)
        for d in range(NSA_HD):
            x = jnp.concatenate([cbuf[c * NSA_KVW + g * NSA_HD + d]
                                 for g in range(NSA_KV_HEADS)], axis=0)
            acc += jnp.dot((x + pet_ref[c, d]).astype(BF16), w2t_ref[c, d],
                           preferred_element_type=F32)
        cmp_.append(jnp.concatenate([acc[g * npages:(g + 1) * npages]
                                     for g in range(NSA_KV_HEADS)], axis=1).astype(BF16))
    ck, cv = cmp_

    pidx = lax.broadcasted_iota(jnp.int32, (nh, npages), 1)
    sc, valid = [], []
    for bb in range(bpp):
        s = lax.dot_general((qc_ref[j, bb] * (NSA_HD ** -0.5)).astype(BF16), ck, dims_t,
                            preferred_element_type=F32)
        v = (bpp * pidx + bb + 1) * BLOCK - 1 <= pos
        sc.append(jnp.where(v, s, NEG))
        valid.append(v)
    mc = functools.reduce(jnp.maximum, [jnp.max(s, axis=1, keepdims=True) for s in sc])
    pc = [jnp.where(v, jnp.exp(s - mc), 0.0) for s, v in zip(sc, valid)]
    lc = functools.reduce(jnp.add, [jnp.sum(p, axis=1, keepdims=True) for p in pc])
    inv_c = 1.0 / jnp.maximum(lc, 1e-30)
    pc = [p * inv_c for p in pc]
    halves = []
    for g in range(NSA_KV_HEADS):
        t = None
        for bb in range(bpp):
            r = jnp.dot(pc[bb].astype(BF16), cv[:, g * NSA_KVW:(g + 1) * NSA_KVW],
                        preferred_element_type=F32)
            r = jnp.where((lane >= bb * BLOCK) & (lane < (bb + 1) * BLOCK), r, 0.0)
            t = r if t is None else t + r
        halves.append(t + pltpu.roll(t, BLOCK, axis=1))
    oc = jnp.where(lane < NSA_HD, halves[0], halves[1])

    width = -(-(nb + 1) // 128) * 128
    imp = [jnp.concatenate([jnp.broadcast_to(
        jnp.sum(p[g * NSA_HPG:(g + 1) * NSA_HPG], axis=0, keepdims=True), (NSA_HPG, npages))
        for g in range(NSA_KV_HEADS)], axis=0) for p in pc]
    imp = jnp.concatenate(imp + [jnp.zeros((nh, width - nb), F32)], axis=1)
    li = lax.broadcasted_iota(jnp.int32, (nh, width), 1)
    assert npages & (npages - 1) == 0
    shift = npages.bit_length() - 1
    bi = jnp.where(li < nb, bpp * jnp.bitwise_and(li, npages - 1)
                   + lax.shift_right_logical(li, shift), li)
    bf = bi.astype(F32)
    causal = bi <= cur
    forced = jnp.logical_and(causal, jnp.logical_or(bi == 0, bi > cur - N_LOCAL))
    score = jnp.where(forced, -4.0, jnp.where(causal, imp, -1.0))
    score = jnp.where(li <= nb, score, -3.0)

    if nb + 1 <= N_SELECT:
        sel = jnp.where(li <= nb, 1.0, 0.0)
    else:
        cols = jnp.concatenate([score, bf, jnp.zeros((128 - 2 * nh, width), F32)], axis=0).T
        earlier = jnp.where(cols[:, nh:nh + 1] < bf[0:1], 1.0, 0.0)
        sels = []
        for g in range(NSA_KV_HEADS):
            scol = cols[:, g * NSA_HPG:g * NSA_HPG + 1]
            srow = score[g * NSA_HPG:g * NSA_HPG + 1]
            ahead = jnp.where(scol > srow, 1.0, jnp.where(scol == srow, earlier, 0.0))
            rank = jnp.sum(ahead, axis=0, keepdims=True)
            take = jnp.logical_and(rank < N_SELECT - 1 - N_LOCAL, srow > -2.0)
            sels.append(jnp.broadcast_to(jnp.where(take, 1.0, 0.0), (NSA_HPG, width)))
        sel = jnp.where(forced, 1.0, jnp.concatenate(sels, axis=0))
    selb = jnp.where(sel > 0.0, 0.0, NEG)
    bias = jnp.dot(selb[:, :nb].astype(BF16), exp_ref[...], preferred_element_type=F32)
    selb_new = selb[:, nb:nb + 1]

    s_parts = []
    for p in range(npages):
        kt = sbuf[p * _HALF_ROWS:p * _HALF_ROWS + NSA_KVW, :].astype(BF16)
        s_parts.append(jnp.dot(qb, kt, preferred_element_type=F32)
                       + bias[:, p * page:(p + 1) * page])
    s_new = jnp.sum(qf * _bf16_round(kvn[:, 2 * NSA_KVW:3 * NSA_KVW]), axis=1,
                    keepdims=True) + selb_new
    p_parts, p_new, inv = _softmax_rows(s_parts, s_new)
    acc = _bf16_round(p_new) * _bf16_round(kvn[:, 3 * NSA_KVW:])
    for p in range(npages):
        vt = sbuf[p * _HALF_ROWS + NSA_KVW:(p + 1) * _HALF_ROWS, :].astype(BF16)
        acc += lax.dot_general(p_parts[p].astype(BF16), vt, dims_t, preferred_element_type=F32)
    os_ = acc * inv

    win = win_ref[j]
    wlen = win.shape[1]
    sw = jnp.dot(qb, win[:NSA_KVW].astype(BF16), preferred_element_type=F32)
    wpos = pos - wlen + lax.broadcasted_iota(jnp.int32, (nh, wlen), 1)
    dist = pos - wpos
    sw = jnp.where(jnp.logical_and(jnp.logical_and(dist >= 0, dist <= WINDOW), wpos >= 0),
                   sw, NEG)
    sw_new = jnp.sum(qf * _bf16_round(wkvn[:, :NSA_KVW]), axis=1, keepdims=True)
    (pw,), pw_new, inv_w = _softmax_rows([sw], sw_new)
    ow = (lax.dot_general(pw.astype(BF16), win[NSA_KVW:].astype(BF16), dims_t,
                          preferred_element_type=F32)
          + _bf16_round(pw_new) * _bf16_round(wkvn[:, NSA_KVW:])) * inv_w

    gt = _sigmoid(gts_ref[j])
    o_ref[j] = gt[:, 0:1] * oc + gt[:, 1:2] * os_ + gt[:, 2:3] * ow

    wl = lax.broadcasted_iota(jnp.int32, win.shape, 1)
    wout_ref[j] = jnp.where(wl == wlen - 1, wcol_ref[j], pltpu.roll(win, wlen - 1, axis=1))


def _nsa_sample_t_kernel(pt_ref, cache_ref, qc_ref, qbd_ref, gts_ref, kvn_ref, wkvn_ref, wcol_ref,
                         win_ref, w2t_ref, pet_ref, exp_ref, o_ref, wout_ref, cbuf0, sbuf0, cbuf1,
                         sbuf1, sem, *, npages, page):
    s = pl.program_id(0)
    buf0 = (cbuf0, sbuf0)
    buf1 = (cbuf1, sbuf1)
    compute = functools.partial(_sample_compute, qc_ref=qc_ref, qbd_ref=qbd_ref, gts_ref=gts_ref,
                                kvn_ref=kvn_ref, wkvn_ref=wkvn_ref, wcol_ref=wcol_ref,
                                win_ref=win_ref, w2t_ref=w2t_ref, pet_ref=pet_ref, exp_ref=exp_ref,
                                o_ref=o_ref, wout_ref=wout_ref, npages=npages, page=page)
    copies = lambda seq, buf, slot: _page_copies(cache_ref, pt_ref, buf, sem.at[slot], seq, npages)

    @pl.when(s == 0)
    def _():
        for c in copies(0, buf0, 0):
            c.start()

    for c in copies(2 * s + 1, buf1, 1):
        c.start()
    for c in copies(2 * s, buf0, 0):
        c.wait()
    compute(buf0, 0)

    @pl.when(s + 1 < pl.num_programs(0))
    def _():
        for c in copies(2 * s + 2, buf0, 0):
            c.start()

    for c in copies(2 * s + 1, buf1, 1):
        c.wait()
    compute(buf1, 1)


def _nsa_sample_t(page_table, cache_t, qc, qbd, gts3, kvn3, wkvn3, wcol, win_t, w2t, pet, expm):
    nd, npages = page_table.shape
    page = cache_t.shape[2]
    wlen = win_t.shape[2]
    assert nd % 2 == 0 and page % BLOCK == 0
    two = lambda *tail: pl.BlockSpec((2,) + tail, lambda s, pt: (s,) + (0,) * len(tail))
    const = lambda a: pl.BlockSpec(a.shape, lambda s, pt: (0,) * a.ndim,
                                   pipeline_mode=pl.Buffered(1))
    grid_spec = pltpu.PrefetchScalarGridSpec(
        num_scalar_prefetch=1,
        grid=(nd // 2,),
        in_specs=[pl.BlockSpec(memory_space=pl.ANY),
                  two(page // BLOCK, NSA_HEADS, 2 * NSA_KVW), two(NSA_HEADS, NSA_KVW),
                  two(NSA_HEADS, NSA_KVW), two(1, 4 * NSA_KVW), two(1, 2 * NSA_KVW),
                  two(2 * NSA_KVW, 1), two(2 * NSA_KVW, wlen), const(w2t), const(pet), const(expm)],
        out_specs=[two(NSA_HEADS, NSA_KVW), two(2 * NSA_KVW, wlen)],
        scratch_shapes=[pltpu.VMEM((_HALF_ROWS, npages, page), F32),
                        pltpu.VMEM((npages * _HALF_ROWS, page), F32)] * 2
                       + [pltpu.SemaphoreType.DMA((2,))])
    return pl.pallas_call(
        functools.partial(_nsa_sample_t_kernel, npages=npages, page=page),
        grid_spec=grid_spec,
        out_shape=[jax.ShapeDtypeStruct((nd, NSA_HEADS, NSA_KVW), F32),
                   jax.ShapeDtypeStruct(win_t.shape, F32)],
        compiler_params=_cparams(("arbitrary",)),
        name="nsa_sample",
    )(page_table.reshape(-1), cache_t, qc, qbd, gts3, kvn3, wkvn3, wcol, win_t, w2t, pet, expm)


def _gla_sample_kernel(gq_ref, gk_ref, gv_ref, misc_ref, w2_ref, ba_ref, s_ref, o_ref, sout_ref,
                       *, ns):
    la = _log_decay(misc_ref[...], w2_ref, ba_ref)
    pad = jnp.zeros((128 - 3 * ns, GLA_KW), F32)
    zt = jnp.concatenate([gq_ref[...] * (GLA_DK ** -0.5), gk_ref[...], jnp.exp(la), pad],
                         axis=0).T
    v = gv_ref[...]
    for i in range(ns):
        outs = []
        for h in range(GLA_HEADS):
            hr = slice(h * GLA_DK, (h + 1) * GLA_DK)
            qc = zt[hr, i:i + 1]
            kc = zt[hr, ns + i:ns + i + 1]
            ac = zt[hr, 2 * ns + i:2 * ns + i + 1]
            s_new = ac * s_ref[i, hr, :] + kc * v[i:i + 1, h * GLA_DV:(h + 1) * GLA_DV]
            sout_ref[i, hr, :] = s_new
            outs.append(jnp.sum(qc * s_new, axis=0, keepdims=True))
        o_ref[i:i + 1, :] = jnp.concatenate(outs, axis=1)


def _gla_sample(gq, gk, gv, misc2d, w2, ba, state3, ns=32):
    nd = gq.shape[0]
    row = lambda w: pl.BlockSpec((ns, w), lambda i: (i, 0))
    st = pl.BlockSpec((ns, GLA_KW, GLA_DV), lambda i: (i, 0, 0))
    return pl.pallas_call(
        functools.partial(_gla_sample_kernel, ns=ns),
        grid=(nd // ns,),
        in_specs=[row(GLA_KW), row(GLA_KW), row(GLA_WIDTH), row(MISC_W),
                  _const_spec(w2.shape), _const_spec(ba.shape), st],
        out_specs=[row(GLA_WIDTH), st],
        out_shape=[jax.ShapeDtypeStruct((nd, GLA_WIDTH), F32),
                   jax.ShapeDtypeStruct(state3.shape, F32)],
        compiler_params=_cparams(("parallel",)),
        name="gla_sample",
    )(gq, gk, gv, misc2d, w2, ba, state3)


def kernel(x_prompt, x_sample, cache_kv, cache_win, state_gla, page_table, p_prompt, p_sample,
           g_attn, w_in, w_cmp, pe_cmp, g_nsa_out, w_gla_a2, b_gla_a, g_gla_out, w_out, g_ffn,
           w_gate_up, w_down, g_ple, w_ple_gate, w_ple_proj, g_final):
    assert g_attn.shape[0] == 1, "single-layer trunk"
    n, s, d = x_prompt.shape
    nd, ds, _ = x_sample.shape
    assert ds == 1 and s % Q_BLOCK == 0 and s >= WINDOW
    w1 = _reorder_w_in(w_in[0])
    fw = _ffn_weights(g_nsa_out[0], g_gla_out[0], w_out[0], g_ffn[0], w_gate_up[0], w_down[0],
                      g_ple[0], w_ple_gate[0], w_ple_proj[0], g_final)
    cw2, cpe = _cmp_weights(w_cmp[0], pe_cmp[0])
    gw2, gba = _gla_gate_weights(w_gla_a2[0], b_gla_a[0])

    xp = x_prompt.reshape(n * s, d)
    q, kv, wkv, gq, gk, gv, gr, misc, kvt, wkvt = _project(xp, g_attn[0], w1, n, min(512, s))
    cmat, ksl, vst, kw, vwt = _nsa_prep(kv, wkv, kvt, wkvt, cw2, cpe, n, s, min(4096, s))
    o_n = _nsa_prompt(q, misc, cmat, ksl, vst, kw, vwt, n, s)
    o_g, s_p = _gla_prompt(gq, gk, gv, misc, gw2, gba, n, s, min(512, s))
    y_p = _mixer_ffn(o_n, o_g, gr, xp, p_prompt[0].reshape(n * s, -1), fw, min(512, n * s))

    xs = x_sample.reshape(nd, d)
    qs, kvs, wkvs, gqs, gks, gvs, grs, miscs, kvts, _ = _project(xs, g_attn[0], w1, 1, nd)
    page = cache_kv.shape[2]
    npages = page_table.shape[1]
    cache_t = _to_channel_major(cache_kv[0])
    win_t = _to_channel_major(cache_win[0])
    q4 = qs.reshape(nd, NSA_KV_HEADS, NSA_HPG, NSA_HD)
    qbd = jnp.concatenate([_place(q4[:, g], g, NSA_KV_HEADS) for g in range(NSA_KV_HEADS)],
                          axis=1)
    bpp = page // BLOCK
    qc = jnp.stack([jnp.concatenate([_place(q4[:, g], g * bpp + bb, NSA_KV_HEADS * bpp)
                                     for g in range(NSA_KV_HEADS)], axis=1)
                    for bb in range(bpp)], axis=1)
    gts3 = jnp.pad(miscs[:, :N_GATES].reshape(nd, NSA_HEADS, 3), ((0, 0), (0, 0), (0, NSA_KVW - 3)))
    w2t, pet, expm = _sample_consts(w_cmp[0], pe_cmp[0], npages, page)
    o8, win_new = _nsa_sample_t(page_table, cache_t, qc, qbd, gts3, kvs.reshape(nd, 1, -1),
                                wkvs.reshape(nd, 1, -1), wkvs.reshape(nd, -1, 1), win_t,
                                w2t, pet, expm)
    o8 = o8.reshape(nd, NSA_KV_HEADS, NSA_HPG, NSA_KV_HEADS, NSA_HD)
    o_ns = jnp.stack([o8[:, 0, :, 0], o8[:, 1, :, 1]], axis=1).reshape(nd, NSA_WIDTH)
    o_gs, s_s = _gla_sample(gqs, gks, gvs, miscs, gw2, gba,
                            state_gla[0].reshape(nd, GLA_KW, GLA_DV))
    y_s = _mixer_ffn(o_ns, o_gs, grs, xs, p_sample[0].reshape(nd, -1), fw, nd)

    wkeep = min(WINDOW, s)
    return (y_p.reshape(n, s, d),
            y_s.reshape(nd, 1, d),
            _from_channel_major(kvt, 4)[None],
            _from_channel_major(wkvt[:, :, s - wkeep:], 2)[None],
            s_p.reshape(1, n, GLA_HEADS, GLA_DK, GLA_DV),
            _from_channel_major(kvts, 4).reshape(1, nd, 1, 4, NSA_KV_HEADS, NSA_HD),
            _from_channel_major(win_new, 2)[None],
            s_s.reshape(1, nd, GLA_HEADS, GLA_DK, GLA_DV))


def _to_channel_major(x):
    n, t = x.shape[:2]
    return jnp.transpose(x, (0, 2, 3, 4, 1)).reshape(n, -1, t)


def _from_channel_major(xt, c):
    n, _, t = xt.shape
    return jnp.transpose(xt.reshape(n, c, NSA_KV_HEADS, NSA_HD, t), (0, 4, 1, 2, 3))


def _place(x, slot, nslots):
    z = jnp.zeros_like(x)
    return jnp.concatenate([x if i == slot else z for i in range(nslots)], axis=-1)
```

```python
import functools

import numpy as np
import jax
import jax.numpy as jnp
from jax import lax
from jax.experimental import pallas as pl
from jax.experimental.pallas import tpu as pltpu

F32 = jnp.float32
BF16 = jnp.bfloat16

EPS = 1e-6
NEG = -1e30
LOG2E = 1.4426950408889634

NSA_HEADS = 8
NSA_KV_HEADS = 2
NSA_HPG = NSA_HEADS // NSA_KV_HEADS
NSA_HD = 64
NSA_WIDTH = NSA_HEADS * NSA_HD
NSA_KVW = NSA_KV_HEADS * NSA_HD
BLOCK = 64
N_SELECT = 16
N_LOCAL = 2
WINDOW = 512
Q_BLOCK = 128
FORCE_SCORE = float(NSA_HPG + 1)
GLA_HEADS = 4
GLA_DK = 64
GLA_DV = 128
GLA_WIDTH = GLA_HEADS * GLA_DV
GLA_KW = GLA_HEADS * GLA_DK
GLA_GATE_RANK = 16
GLA_GATE_TEMP = 16.0
GLA_CHUNK = 64
N_GATES = 3 * NSA_HEADS
MISC_W = 128

VMEM_LIMIT = 56 * 1024 * 1024


def _cparams(sem):
    return pltpu.CompilerParams(dimension_semantics=sem, vmem_limit_bytes=VMEM_LIMIT)


def _const_spec(shape):
    nd = len(shape)
    return pl.BlockSpec(shape, lambda *_: (0,) * nd, pipeline_mode=pl.Buffered(1))


def _sigmoid(x):
    return 1.0 / (1.0 + jnp.exp(-x))


def _rms(x, g):
    ms = jnp.mean(x * x, axis=-1, keepdims=True)
    return x * lax.rsqrt(ms + EPS) * g


def _split2(x):
    hi = x.astype(BF16)
    return hi, (x - hi.astype(F32)).astype(BF16)


_P_Q, _P_KV, _P_WKV, _P_GQ, _P_GK, _P_GV, _P_GR, _P_MISC, _P_END = (
    0, 512, 1024, 1280, 1536, 1792, 2304, 2816, 2944)


def _reorder_w_in(w_in):
    offs = np.cumsum([0, NSA_WIDTH, 4 * NSA_KVW, 2 * NSA_KVW, N_GATES, GLA_KW, GLA_KW,
                      GLA_WIDTH, GLA_GATE_RANK, GLA_WIDTH])
    q, kv, wkv, gts, gq, gk, gv, ga, gr = [w_in[:, offs[i]:offs[i + 1]] for i in range(9)]
    pad = jnp.zeros((w_in.shape[0], MISC_W - N_GATES - GLA_GATE_RANK), w_in.dtype)
    return jnp.concatenate([q, kv, wkv, gq, gk, gv, gr, gts, ga, pad], axis=1).astype(BF16)


def _proj_kernel(x_ref, g_ref, w_ref, wt_ref, q_ref, kv_ref, wkv_ref, gq_ref, gk_ref, gv_ref,
                 gr_ref, misc_ref, kvt_ref, wkvt_ref):
    xn = _rms(x_ref[...], g_ref[...]).astype(BF16)
    for ref, a, b in ((q_ref, _P_Q, _P_KV), (kv_ref, _P_KV, _P_WKV), (wkv_ref, _P_WKV, _P_GQ),
                      (gq_ref, _P_GQ, _P_GK), (gk_ref, _P_GK, _P_GV), (gv_ref, _P_GV, _P_GR),
                      (gr_ref, _P_GR, _P_MISC), (misc_ref, _P_MISC, _P_END)):
        ref[...] = jnp.dot(xn, w_ref[:, a:b], preferred_element_type=F32)
    dims_t = (((1,), (1,)), ((), ()))
    nkv = kvt_ref.shape[1]
    kvt_ref[0] = lax.dot_general(wt_ref[0:nkv, :], xn, dims_t, preferred_element_type=F32)
    wkvt_ref[0] = lax.dot_general(wt_ref[nkv:, :], xn, dims_t, preferred_element_type=F32)


def _project(x2d, g_attn, w1, n, tm):
    t, d = x2d.shape
    s = t // n
    nj = s // tm
    widths = (512, 512, 256, 256, 256, 512, 512, MISC_W)
    wt = w1[:, _P_KV:_P_GQ].T
    tspec = lambda rows: pl.BlockSpec((1, rows, tm), lambda i: (i // nj, 0, i % nj))
    return pl.pallas_call(
        _proj_kernel,
        grid=(t // tm,),
        in_specs=[pl.BlockSpec((tm, d), lambda i: (i, 0)),
                  _const_spec((1, d)),
                  _const_spec(w1.shape),
                  _const_spec(wt.shape)],
        out_specs=[pl.BlockSpec((tm, w), lambda i: (i, 0)) for w in widths]
                  + [tspec(4 * NSA_KVW), tspec(2 * NSA_KVW)],
        out_shape=[jax.ShapeDtypeStruct((t, w), F32) for w in widths]
                  + [jax.ShapeDtypeStruct((n, 4 * NSA_KVW, s), F32),
                     jax.ShapeDtypeStruct((n, 2 * NSA_KVW, s), F32)],
        compiler_params=_cparams(("parallel",)),
        name="proj",
    )(x2d, g_attn.reshape(1, d), w1, wt)


def _group_mean_sq(x, bd_ref, width):
    hi, lo = _split2(x * x)
    bd = bd_ref[...]
    s = jnp.dot(hi, bd, preferred_element_type=F32) + jnp.dot(lo, bd, preferred_element_type=F32)
    return s * (1.0 / width)


def _ffn_kernel(on_ref, og_ref, gr_ref, x_ref, p_ref, gn_ref, gg_ref, bd64_ref, bd128_ref,
                wo_ref, gffn_ref, wgu_ref, wd_ref, gple_ref, wpg_ref, wpp_ref, gfin_ref,
                y_ref, h_ref, acc_ref, *, n_ff):
    o_n = on_ref[...]
    o_g = og_ref[...]
    r = gr_ref[...]
    a_n = o_n * lax.rsqrt(_group_mean_sq(o_n, bd64_ref, NSA_HD) + EPS) * gn_ref[...]
    a_g = o_g * lax.rsqrt(_group_mean_sq(o_g, bd128_ref, GLA_DV) + EPS) * gg_ref[...]
    a_g = a_g * (r * _sigmoid(r))
    x1 = (x_ref[...]
          + jnp.dot(a_n.astype(BF16), wo_ref[0:NSA_WIDTH, :], preferred_element_type=F32)
          + jnp.dot(a_g.astype(BF16), wo_ref[NSA_WIDTH:, :], preferred_element_type=F32))
    h_ref[...] = _rms(x1, gffn_ref[...]).astype(BF16)
    acc_ref[...] = x1

    def body(c, carry):
        gu = jnp.dot(h_ref[...], wgu_ref[c], preferred_element_type=F32)
        half = gu.shape[1] // 2
        gt = gu[:, :half]
        act = (gt * _sigmoid(gt) * gu[:, half:]).astype(BF16)
        acc_ref[...] += jnp.dot(act, wd_ref[c], preferred_element_type=F32)
        return carry

    lax.fori_loop(0, n_ff, body, 0)
    x2 = acc_ref[...]
    gate = _sigmoid(jnp.dot(_rms(x2, gple_ref[...]).astype(BF16), wpg_ref[...],
                            preferred_element_type=F32))
    x3 = x2 + jnp.dot(p_ref[...].astype(BF16), wpp_ref[...], preferred_element_type=F32) * gate
    y_ref[...] = _rms(x3, gfin_ref[...])


def _block_diag_ones(n, width):
    idx = np.arange(n) // width
    return jnp.asarray((idx[:, None] == idx[None, :]).astype(np.float32), BF16)


def _ffn_weights(g_nsa_out, g_gla_out, w_out, g_ffn, w_gate_up, w_down, g_ple, w_ple_gate,
                 w_ple_proj, g_final, ff_chunk=256):
    d = w_out.shape[1]
    d_ff = w_down.shape[0]
    n_ff = d_ff // ff_chunk
    wg = w_gate_up[:, :d_ff].reshape(d, n_ff, ff_chunk)
    wu = w_gate_up[:, d_ff:].reshape(d, n_ff, ff_chunk)
    wgu = jnp.concatenate([wg, wu], axis=2).transpose(1, 0, 2).astype(BF16)
    wd = w_down.reshape(n_ff, ff_chunk, d).astype(BF16)
    return dict(gn=g_nsa_out.reshape(1, NSA_WIDTH), gg=g_gla_out.reshape(1, GLA_WIDTH),
                bd64=_block_diag_ones(NSA_WIDTH, NSA_HD), bd128=_block_diag_ones(GLA_WIDTH, GLA_DV),
                wo=w_out.astype(BF16), gffn=g_ffn.reshape(1, d), wgu=wgu, wd=wd,
                gple=g_ple.reshape(1, d), wpg=w_ple_gate.astype(BF16),
                wpp=w_ple_proj.astype(BF16), gfin=g_final.reshape(1, d))


def _mixer_ffn(o_n, o_g, gr, x2d, p2d, fw, tm):
    t, d = x2d.shape
    n_ff = fw["wgu"].shape[0]
    consts = [fw[k] for k in ("gn", "gg", "bd64", "bd128", "wo", "gffn", "wgu", "wd", "gple",
                              "wpg", "wpp", "gfin")]
    row = lambda w: pl.BlockSpec((tm, w), lambda i: (i, 0))
    return pl.pallas_call(
        functools.partial(_ffn_kernel, n_ff=n_ff),
        grid=(t // tm,),
        in_specs=[row(NSA_WIDTH), row(GLA_WIDTH), row(GLA_WIDTH), row(d), row(p2d.shape[1])]
                 + [_const_spec(c.shape) for c in consts],
        out_specs=row(d),
        out_shape=jax.ShapeDtypeStruct((t, d), F32),
        scratch_shapes=[pltpu.VMEM((tm, d), BF16), pltpu.VMEM((tm, d), F32)],
        compiler_params=_cparams(("parallel",)),
        name="mixer_ffn",
    )(o_n, o_g, gr, x2d, p2d, *consts)


def _cmp_weights(w_cmp, pe_cmp):
    z = jnp.zeros((BLOCK, NSA_HD, NSA_HD), w_cmp.dtype)

    def bd(w):
        return jnp.concatenate([jnp.concatenate([w, z], axis=2),
                                jnp.concatenate([z, w], axis=2)], axis=1)

    w2 = jnp.stack([bd(w_cmp[0]), bd(w_cmp[1])], axis=1).astype(BF16)
    pe = jnp.stack([jnp.concatenate([pe_cmp[0], pe_cmp[0]], axis=1),
                    jnp.concatenate([pe_cmp[1], pe_cmp[1]], axis=1)], axis=1)
    return w2, pe.reshape(BLOCK, 2, 1, NSA_KVW)


def _compress(xk_ref, xv_ref, w2_ref, pe_ref, nb):
    acc_k = jnp.zeros((nb, NSA_KVW), F32)
    acc_v = jnp.zeros((nb, NSA_KVW), F32)
    for l in range(BLOCK):
        xk = (xk_ref[pl.ds(l, nb, stride=BLOCK), :] + pe_ref[l, 0]).astype(BF16)
        xv = (xv_ref[pl.ds(l, nb, stride=BLOCK), :] + pe_ref[l, 1]).astype(BF16)
        acc_k += jnp.dot(xk, w2_ref[l, 0], preferred_element_type=F32)
        acc_v += jnp.dot(xv, w2_ref[l, 1], preferred_element_type=F32)
    return acc_k, acc_v


def _prep_kernel(ckin_ref, cvin_ref, ksin_ref, kwin_ref, vst_in_ref, vwt_in_ref, w2_ref, pe_ref,
                 c_ref, ks_ref, vst_ref, kw_ref, vwt_ref, *, tk):
    nb = tk // BLOCK
    ck, cv = _compress(ckin_ref, cvin_ref, w2_ref, pe_ref, nb)
    c_ref[:, 0:NSA_KVW] = ck
    c_ref[:, NSA_KVW:] = cv
    ks_ref[...] = ksin_ref[...].astype(BF16)
    kw_ref[...] = kwin_ref[...].astype(BF16)
    vst_ref[...] = vst_in_ref[...].astype(BF16)
    vwt_ref[...] = vwt_in_ref[...].astype(BF16)


def _nsa_prep(kv2d, wkv2d, kvt, wkvt, w2, pe, n, s, tk):
    t = n * s
    nj = s // tk
    kv_specs = [pl.BlockSpec((tk, NSA_KVW), (lambda b, j, c=c: (b * nj + j, c))) for c in range(3)]
    return pl.pallas_call(
        functools.partial(_prep_kernel, tk=tk),
        grid=(n, nj),
        in_specs=kv_specs + [pl.BlockSpec((tk, NSA_KVW), lambda b, j: (b * nj + j, 0)),
                             pl.BlockSpec((1, NSA_KVW, tk), lambda b, j: (b, 3, j)),
                             pl.BlockSpec((1, NSA_KVW, tk), lambda b, j: (b, 1, j)),
                             _const_spec(w2.shape), _const_spec(pe.shape)],
        out_specs=[pl.BlockSpec((tk // BLOCK, 2 * NSA_KVW), lambda b, j: (b * nj + j, 0)),
                   pl.BlockSpec((tk, NSA_KVW), lambda b, j: (b * nj + j, 0)),
                   pl.BlockSpec((1, NSA_KVW, tk), lambda b, j: (b, 0, j)),
                   pl.BlockSpec((tk, NSA_KVW), lambda b, j: (b * nj + j, 0)),
                   pl.BlockSpec((1, NSA_KVW, tk), lambda b, j: (b, 0, j))],
        out_shape=[jax.ShapeDtypeStruct((t // BLOCK, 2 * NSA_KVW), F32),
                   jax.ShapeDtypeStruct((t, NSA_KVW), BF16),
                   jax.ShapeDtypeStruct((n, NSA_KVW, s), BF16),
                   jax.ShapeDtypeStruct((t, NSA_KVW), BF16),
                   jax.ShapeDtypeStruct((n, NSA_KVW, s), BF16)],
        compiler_params=_cparams(("parallel", "parallel")),
        name="nsa_prep",
    )(kv2d, kv2d, kv2d, wkv2d, kvt, wkvt, w2, pe)


_LQ = NSA_KV_HEADS * NSA_HPG * Q_BLOCK
_LG = NSA_HPG * Q_BLOCK
_SEL_SHIFT = 2
_SEL_GROUP = Q_BLOCK << _SEL_SHIFT


def _rep_heads(x):
    a, b = x[:, :Q_BLOCK], x[:, Q_BLOCK:]
    return jnp.concatenate([a] * NSA_HPG + [b] * NSA_HPG, axis=1)


def _flash_update(s, vt, carry):
    m, l, acc0, acc1 = carry
    m_new = jnp.maximum(m, jnp.max(s, axis=0, keepdims=True))
    alpha = jnp.exp2(m - m_new)
    p = jnp.exp2(s - m_new)
    l = alpha * l + jnp.sum(p, axis=0, keepdims=True)
    pb = p.astype(BF16)
    acc0 = alpha[:, :_LG] * acc0 + jnp.dot(vt[:NSA_HD], pb[:, :_LG], preferred_element_type=F32)
    acc1 = alpha[:, _LG:] * acc1 + jnp.dot(vt[NSA_HD:], pb[:, _LG:], preferred_element_type=F32)
    return m_new, l, acc0, acc1


def _flash_init():
    return (jnp.full((1, _LQ), NEG, F32), jnp.zeros((1, _LQ), F32),
            jnp.zeros((NSA_HD, _LG), F32), jnp.zeros((NSA_HD, _LG), F32))


def _flash_out(carry):
    m, l, acc0, acc1 = carry
    inv = 1.0 / jnp.maximum(l, 1e-30)
    return acc0 * inv[:, :_LG], acc1 * inv[:, _LG:]


def _select_blocks(imps, cur, nbs):
    if nbs <= N_SELECT:
        return [jnp.ones(imp.shape, F32) for imp in imps]
    bi = lax.broadcasted_iota(jnp.int32, imps[0].shape, 0)
    bf = bi.astype(F32)
    causal = bi <= cur
    forced = jnp.logical_and(causal, jnp.logical_or(bi == 0, bi > cur - N_LOCAL))
    scores = tuple(jnp.where(forced, -2.0, jnp.where(causal, imp, -1.0)) for imp in imps)

    def take_one(score):
        mx = jnp.max(score, axis=0, keepdims=True)
        idx = jnp.min(jnp.where(score == mx, bf, float(nbs)), axis=0, keepdims=True)
        return jnp.where(bf == idx, -2.0, score)

    scores = lax.fori_loop(0, N_SELECT - 1 - N_LOCAL,
                           lambda _, sc: tuple(take_one(s) for s in sc), scores)
    return [jnp.where(s == -2.0, 1.0, 0.0) for s in scores]


def _nsa_prompt_kernel(q_ref, misc_ref, c_ref, ks_ref, vst_ref, ind_ref, indw_ref, *rest, nbc):
    kw_refs = rest[0:5]
    vwt_refs = rest[5:10]
    o_ref = rest[10]
    selb_ref, s0_ref, s1_ref, p0_ref, p1_ref = rest[11:16]
    i = pl.program_id(1)

    qT = (q_ref[...] * (NSA_HD ** -0.5 * LOG2E)).T
    z = jnp.zeros((NSA_HD, Q_BLOCK), F32)
    top = [qT[NSA_HD * h:NSA_HD * (h + 1)] for h in range(NSA_HPG)] + [z] * NSA_HPG
    bot = [z] * NSA_HPG + [qT[NSA_HD * h:NSA_HD * (h + 1)] for h in range(NSA_HPG, NSA_HEADS)]
    qt = jnp.concatenate([jnp.concatenate(top, axis=1), jnp.concatenate(bot, axis=1)],
                         axis=0).astype(BF16)

    cmat = c_ref[...]
    ck = cmat[:, :NSA_KVW].astype(BF16)
    cv = cmat[:, NSA_KVW:].astype(BF16)
    sc = jnp.dot(ck, qt, preferred_element_type=F32)
    b_io = lax.broadcasted_iota(jnp.int32, (nbc, _LQ), 0)
    qoff = jnp.bitwise_and(lax.broadcasted_iota(jnp.int32, (nbc, _LQ), 1), Q_BLOCK - 1)
    valid = b_io * BLOCK + (BLOCK - 1) <= i * Q_BLOCK + qoff
    sc = jnp.where(valid, sc, NEG)
    mc = jnp.max(sc, axis=0, keepdims=True)
    pc = jnp.where(valid, jnp.exp2(sc - mc), 0.0)
    pc = pc / jnp.maximum(jnp.sum(pc, axis=0, keepdims=True), 1e-30)
    oc_full = lax.dot_general(cv, pc.astype(BF16), (((0,), (0,)), ((), ())),
                              preferred_element_type=F32)
    oc = (oc_full[:NSA_HD, :_LG], oc_full[NSA_HD:, _LG:])

    qo = lax.broadcasted_iota(jnp.int32, (1, Q_BLOCK), 1)
    cur = 2 * i + (qo >= BLOCK).astype(jnp.int32)
    imps = []
    for g in range(NSA_KV_HEADS):
        imp = pc[:, g * _LG:g * _LG + Q_BLOCK]
        for hh in range(1, NSA_HPG):
            imp = imp + pc[:, g * _LG + hh * Q_BLOCK:g * _LG + (hh + 1) * Q_BLOCK]
        imps.append(imp)
    for g, sel in enumerate(_select_blocks(imps, cur, nbc)):
        selb_ref[:, g * Q_BLOCK:(g + 1) * Q_BLOCK] = jnp.where(sel > 0.0, 0.0, NEG)

    nbg = _SEL_GROUP // BLOCK
    ind = ind_ref[...]
    zpad = jnp.zeros((NSA_KVW - 16, _LQ), BF16)

    def scores(jg):
        off = pl.multiple_of(jg * _SEL_GROUP, _SEL_GROUP)
        ka = jnp.concatenate([ks_ref[pl.ds(off, _SEL_GROUP), :], ind], axis=1)
        brow = selb_ref[pl.ds(pl.multiple_of(jg * nbg, nbg), nbg), :]
        baug = jnp.concatenate([_rep_heads(brow), jnp.zeros((16 - nbg, _LQ), F32)],
                               axis=0).astype(BF16)
        return jnp.dot(ka, jnp.concatenate([qt, baug, zpad], axis=0), preferred_element_type=F32)

    def values(jg, p_ref):
        vt = vst_ref[0, :, pl.ds(pl.multiple_of(jg * _SEL_GROUP, _SEL_GROUP), _SEL_GROUP)]
        return (jnp.dot(vt[:NSA_HD], p_ref[:, :_LG], preferred_element_type=F32),
                jnp.dot(vt[NSA_HD:], p_ref[:, _LG:], preferred_element_type=F32))

    def stage(jg, s_cur, p_cur, s_nxt, p_prev, st, diagonal=False):
        m, l, b0, b1 = st
        if s_nxt is not None:
            s_nxt[...] = scores(jg + 1)
        pv0, pv1 = values(jnp.maximum(jg - 1, 0), p_prev)
        s = s_cur[...]
        if diagonal:
            rowg = lax.broadcasted_iota(jnp.int32, (_SEL_GROUP, _LQ), 0)
            qcg = jnp.bitwise_and(lax.broadcasted_iota(jnp.int32, (_SEL_GROUP, _LQ), 1),
                                  Q_BLOCK - 1)
            s = jnp.where(jg * _SEL_GROUP + rowg <= i * Q_BLOCK + qcg, s, NEG)
        m_new = jnp.maximum(m, jnp.max(s, axis=0, keepdims=True))
        alpha = jnp.exp2(m - m_new)
        p = jnp.exp2(s - m_new)
        p_cur[...] = p.astype(BF16)
        l = alpha * l + jnp.sum(p, axis=0, keepdims=True)
        return m_new, l, alpha[:, :_LG] * (b0 + pv0), alpha[:, _LG:] * (b1 + pv1)

    def finish(st, p_last, jg):
        m, l, b0, b1 = st
        pv0, pv1 = values(jg, p_last)
        inv = 1.0 / jnp.maximum(l, 1e-30)
        return (b0 + pv0) * inv[:, :_LG], (b1 + pv1) * inv[:, _LG:]

    nfull = lax.shift_right_logical(i, _SEL_SHIFT)
    s0_ref[...] = scores(0)
    p1_ref[...] = jnp.zeros(p1_ref.shape, BF16)

    def pair(t, st):
        st = stage(2 * t, s0_ref, p0_ref, s1_ref, p1_ref, st)
        return stage(2 * t + 1, s1_ref, p1_ref, s0_ref, p0_ref, st)

    st = lax.fori_loop(0, lax.shift_right_logical(nfull, 1), pair, _flash_init())

    def odd_tail(st):
        st = stage(nfull - 1, s0_ref, p0_ref, s1_ref, p1_ref, st)
        st = stage(nfull, s1_ref, p1_ref, None, p0_ref, st, diagonal=True)
        return finish(st, p1_ref, nfull)

    def even_tail(st):
        st = stage(nfull, s0_ref, p0_ref, None, p1_ref, st, diagonal=True)
        return finish(st, p0_ref, nfull)

    os_ = lax.cond(jnp.bitwise_and(nfull, 1) == 1, odd_tail, even_tail, st)

    nwt = len(kw_refs)
    row = lax.broadcasted_iota(jnp.int32, (Q_BLOCK, _LQ), 0)
    qcol = jnp.bitwise_and(lax.broadcasted_iota(jnp.int32, (Q_BLOCK, _LQ), 1), Q_BLOCK - 1)
    tile = lax.broadcasted_iota(jnp.int32, (16, _LQ), 0)
    wbias = jnp.where(tile < nwt - 1 - i, NEG, 0.0).astype(BF16)
    kwa = jnp.concatenate([jnp.concatenate([r[...] for r in kw_refs], axis=0), indw_ref[...]],
                          axis=1)
    sw = jnp.dot(kwa, jnp.concatenate([qt, wbias, zpad], axis=0), preferred_element_type=F32)
    sw = jnp.concatenate([jnp.where(row >= qcol, sw[:Q_BLOCK], NEG),
                          sw[Q_BLOCK:(nwt - 1) * Q_BLOCK],
                          jnp.where(row <= qcol, sw[(nwt - 1) * Q_BLOCK:], NEG)], axis=0)
    ow = _flash_out(_flash_update(sw, jnp.concatenate([r[0] for r in vwt_refs], axis=1),
                                  _flash_init()))

    gt = _sigmoid(misc_ref[...]).T
    outs = []
    for g in range(NSA_KV_HEADS):
        for hh in range(NSA_HPG):
            h = g * NSA_HPG + hh
            sl = slice(hh * Q_BLOCK, (hh + 1) * Q_BLOCK)
            outs.append(gt[3 * h:3 * h + 1] * oc[g][:, sl] + gt[3 * h + 1:3 * h + 2] * os_[g][:, sl]
                        + gt[3 * h + 2:3 * h + 3] * ow[g][:, sl])
    o_ref[...] = jnp.concatenate(outs, axis=0).T


def _nsa_prompt(q2d, misc2d, cmat, ks, vst, kw, vwt, n, s):
    assert s % _SEL_GROUP == 0
    nq = s // Q_BLOCK
    nbc = s // BLOCK
    blk = np.arange(_SEL_GROUP) // BLOCK
    ind = jnp.asarray((blk[:, None] == np.arange(NSA_KVW)[None, :]).astype(np.float32), BF16)
    wtile = np.arange(5 * Q_BLOCK) // Q_BLOCK
    indw = jnp.asarray((wtile[:, None] == np.arange(NSA_KVW)[None, :]).astype(np.float32), BF16)
    wk_specs = [pl.BlockSpec((Q_BLOCK, NSA_KVW),
                             (lambda b, i, t=t: (b * nq + jnp.maximum(i - 4 + t, 0), 0)))
                for t in range(5)]
    wv_specs = [pl.BlockSpec((1, NSA_KVW, Q_BLOCK),
                             (lambda b, i, t=t: (b, 0, jnp.maximum(i - 4 + t, 0))))
                for t in range(5)]
    return pl.pallas_call(
        functools.partial(_nsa_prompt_kernel, nbc=nbc),
        grid=(n, nq),
        in_specs=[pl.BlockSpec((Q_BLOCK, NSA_WIDTH), lambda b, i: (b * nq + i, 0)),
                  pl.BlockSpec((Q_BLOCK, MISC_W), lambda b, i: (b * nq + i, 0)),
                  pl.BlockSpec((nbc, 2 * NSA_KVW), lambda b, i: (b, 0)),
                  pl.BlockSpec((s, NSA_KVW), lambda b, i: (b, 0)),
                  pl.BlockSpec((1, NSA_KVW, s), lambda b, i: (b, 0, 0)),
                  _const_spec(ind.shape), _const_spec(indw.shape)] + wk_specs + wv_specs,
        out_specs=pl.BlockSpec((Q_BLOCK, NSA_WIDTH), lambda b, i: (b * nq + i, 0)),
        out_shape=jax.ShapeDtypeStruct((n * s, NSA_WIDTH), F32),
        scratch_shapes=[pltpu.VMEM((nbc, 2 * Q_BLOCK), F32),
                        pltpu.VMEM((_SEL_GROUP, _LQ), F32), pltpu.VMEM((_SEL_GROUP, _LQ), F32),
                        pltpu.VMEM((_SEL_GROUP, _LQ), BF16), pltpu.VMEM((_SEL_GROUP, _LQ), BF16)],
        compiler_params=_cparams(("parallel", "arbitrary")),
        name="nsa_prompt",
    )(q2d, misc2d, cmat, ks, vst, ind, indw, *([kw] * 5), *([vwt] * 5))


_N_LEVELS = int(np.log2(GLA_CHUNK))


def _gla_consts():
    c = GLA_CHUNK
    t = np.arange(c)
    mats = [t[None, :] <= t[:, None], t[None, :] > t[:, None]]
    masks = []
    w = c // 2
    while w >= 1:
        blk = t // (2 * w)
        mid = blk * 2 * w + w
        upper = t >= mid
        m = np.zeros((c, c), bool)
        for r in range(c):
            if upper[r]:
                m[r, mid[r]:r + 1] = True
            else:
                m[r, r + 1:mid[r]] = True
        mats.append(m)
        masks.append((blk[:, None] == blk[None, :]) & upper[:, None] & ~upper[None, :])
        w //= 2
    masks.append(np.eye(c, dtype=bool))
    mall = jnp.asarray(np.concatenate(mats, axis=0).astype(np.float32), BF16)
    lmask = jnp.asarray(np.tile(np.stack(masks).astype(np.float32), (1, GLA_HEADS, 1)))
    hd = np.arange(GLA_KW) // GLA_DK
    hmask = jnp.asarray((hd[:, None] == hd[None, :]).astype(np.float32))
    return mall, lmask, hmask


def _gla_gate_weights(w_a2, b_a):
    w2 = jnp.zeros((MISC_W, GLA_KW), F32).at[N_GATES:N_GATES + GLA_GATE_RANK].set(w_a2)
    return w2.astype(BF16), b_a.reshape(1, GLA_KW)


def _log_decay(misc, w2_ref, ba_ref):
    x = jnp.dot(misc.astype(BF16), w2_ref[...], preferred_element_type=F32) + ba_ref[...]
    return (jnp.minimum(x, 0.0) - jnp.log1p(jnp.exp(-jnp.abs(x)))) * (1.0 / GLA_GATE_TEMP)


def _gla_kernel(gq_ref, gk_ref, gv_ref, misc_ref, w2_ref, ba_ref, mall_ref, lmask_ref, hm_ref,
                o_ref, sout_ref, s_ref, *, nchunk):
    j = pl.program_id(1)

    @pl.when(j == 0)
    def _():
        s_ref[...] = jnp.zeros(s_ref.shape, F32)

    la_all = _log_decay(misc_ref[...], w2_ref, ba_ref)
    mall = mall_ref[...]
    hm = hm_ref[...]
    c = GLA_CHUNK
    dims_t = (((1,), (1,)), ((), ()))

    def stack(x):
        return (jnp.concatenate([x] * GLA_HEADS, axis=0) * hm).astype(BF16)

    for ci in range(nchunk):
        rows = slice(ci * c, (ci + 1) * c)
        q = gq_ref[rows, :] * (GLA_DK ** -0.5)
        k = gk_ref[rows, :]
        v = gv_ref[rows, :].astype(BF16)
        la = la_all[rows]
        hi, lo = _split2(la)
        ex = jnp.exp(jnp.dot(mall, hi, preferred_element_type=F32)
                     + jnp.dot(mall, lo, preferred_element_type=F32))
        att = lax.dot_general(stack(q), k.astype(BF16), dims_t,
                              preferred_element_type=F32) * lmask_ref[_N_LEVELS]
        for lev in range(_N_LEVELS):
            e = ex[(2 + lev) * c:(3 + lev) * c]
            att += lax.dot_general(stack(q * e), (k * e).astype(BF16), dims_t,
                                   preferred_element_type=F32) * lmask_ref[lev]
        attb = att.astype(BF16)
        s_old = s_ref[...]
        o_inter = jnp.dot(stack(q * ex[0:c]), s_old.astype(BF16), preferred_element_type=F32)
        kd = (k * ex[c:2 * c]).astype(BF16)
        upd = lax.dot_general(kd, v, (((0,), (0,)), ((), ())), preferred_element_type=F32)
        dec = jnp.exp(jnp.sum(la.T, axis=1, keepdims=True))
        outs, news = [], []
        for h in range(GLA_HEADS):
            hr = slice(h * GLA_DK, (h + 1) * GLA_DK)
            hv = slice(h * GLA_DV, (h + 1) * GLA_DV)
            outs.append(o_inter[hr] + jnp.dot(attb[hr], v[:, hv], preferred_element_type=F32))
            news.append(upd[hr, hv])
        o_ref[rows, :] = jnp.concatenate(outs, axis=1)
        s_ref[...] = dec * s_old + jnp.concatenate(news, axis=0)

    @pl.when(j == pl.num_programs(1) - 1)
    def _():
        sout_ref[0] = s_ref[...]


def _gla_prompt(gq, gk, gv, misc2d, w2, ba, n, s, tc):
    nj = s // tc
    mall, lmask, hmask = _gla_consts()
    row = lambda w: pl.BlockSpec((tc, w), lambda b, j: (b * nj + j, 0))
    return pl.pallas_call(
        functools.partial(_gla_kernel, nchunk=tc // GLA_CHUNK),
        grid=(n, nj),
        in_specs=[row(GLA_KW), row(GLA_KW), row(GLA_WIDTH), row(MISC_W),
                  _const_spec(w2.shape), _const_spec(ba.shape), _const_spec(mall.shape),
                  _const_spec(lmask.shape), _const_spec(hmask.shape)],
        out_specs=[row(GLA_WIDTH), pl.BlockSpec((1, GLA_KW, GLA_DV), lambda b, j: (b, 0, 0))],
        out_shape=[jax.ShapeDtypeStruct((n * s, GLA_WIDTH), F32),
                   jax.ShapeDtypeStruct((n, GLA_KW, GLA_DV), F32)],
        scratch_shapes=[pltpu.VMEM((GLA_KW, GLA_DV), F32)],
        compiler_params=_cparams(("parallel", "arbitrary")),
        name="gla_prompt",
    )(gq, gk, gv, misc2d, w2, ba, mall, lmask, hmask)


def _softmax_rows(s_parts, s_new):
    m = s_new
    for s in s_parts:
        m = jnp.maximum(m, jnp.max(s, axis=1, keepdims=True))
    p_parts = [jnp.exp(s - m) for s in s_parts]
    p_new = jnp.exp(s_new - m)
    l = p_new
    for p in p_parts:
        l = l + jnp.sum(p, axis=1, keepdims=True)
    return p_parts, p_new, 1.0 / jnp.maximum(l, 1e-30)


def _bf16_round(x):
    return x.astype(BF16).astype(F32)


_HALF_ROWS = 2 * NSA_KVW


def _page_copies(cache_ref, pt_ref, bufs, sem, seq, npages):
    cbuf, sbuf = bufs
    copies = []
    for p in range(npages):
        pg = pt_ref[seq * npages + p]
        copies.append(pltpu.make_async_copy(cache_ref.at[pg, pl.ds(0, _HALF_ROWS), :],
                                            cbuf.at[:, p, :], sem))
        copies.append(pltpu.make_async_copy(cache_ref.at[pg, pl.ds(_HALF_ROWS, _HALF_ROWS), :],
                                            sbuf.at[pl.ds(p * _HALF_ROWS, _HALF_ROWS), :], sem))
    return copies


def _sample_consts(w_cmp, pe_cmp, npages, page):
    bpp = page // BLOCK
    z = jnp.zeros((BLOCK, NSA_HD, NSA_HD), w_cmp.dtype)
    per_c = []
    for c in range(2):
        w = w_cmp[c].transpose(1, 0, 2)
        rows = [jnp.concatenate([w if b2 == b1 else z for b2 in range(bpp)], axis=2)
                for b1 in range(bpp)]
        per_c.append(jnp.concatenate(rows, axis=1))
    w2t = jnp.stack(per_c).astype(BF16)
    pet = jnp.stack([jnp.tile(pe_cmp[c].T, (1, bpp)) for c in range(2)])
    pet = pet.reshape(2, NSA_HD, 1, page)
    j = np.arange(bpp * npages)
    col = np.arange(npages * page)
    expm = (j[:, None] == ((col % page) // BLOCK) * npages + col // page)
    return w2t, pet, jnp.asarray(expm.astype(np.float32), BF16)


def _sample_compute(bufs, j, qc_ref, qbd_ref, gts_ref, kvn_ref, wkvn_ref, wcol_ref, win_ref,
                    w2t_ref, pet_ref, exp_ref, o_ref, wout_ref, *, npages, page):
    cbuf, sbuf = bufs
    bpp = page // BLOCK
    nb = npages * bpp
    pos = npages * page
    cur = pos // BLOCK
    nh = NSA_HEADS
    dims_t = (((1,), (1,)), ((), ()))
    lane = lax.broadcasted_iota(jnp.int32, (nh, NSA_KVW), 1)
    qf = _bf16_round(qbd_ref[j] * (NSA_HD ** -0.5))
    qb = qf.astype(BF16)
    kvn = kvn_ref[j]
    wkvn = wkvn_ref[j]

    cmp_ = []
    for c in range(2):
        acc = jnp.zeros((NSA_KV_HEADS * npages, page), F32)
        for d in range(NSA_HD):
            x = jnp.concatenate([cbuf[c * NSA_KVW + g * NSA_HD + d]
                                 for g in range(NSA_KV_HEADS)], axis=0)
            acc += jnp.dot((x + pet_ref[c, d]).astype(BF16), w2t_ref[c, d],
                           preferred_element_type=F32)
        cmp_.append(jnp.concatenate([acc[g * npages:(g + 1) * npages]
                                     for g in range(NSA_KV_HEADS)], axis=1).astype(BF16))
    ck, cv = cmp_

    pidx = lax.broadcasted_iota(jnp.int32, (nh, npages), 1)
    sc, valid = [], []
    for bb in range(bpp):
        s = lax.dot_general((qc_ref[j, bb] * (NSA_HD ** -0.5)).astype(BF16), ck, dims_t,
                            preferred_element_type=F32)
        v = (bpp * pidx + bb + 1) * BLOCK - 1 <= pos
        sc.append(jnp.where(v, s, NEG))
        valid.append(v)
    mc = functools.reduce(jnp.maximum, [jnp.max(s, axis=1, keepdims=True) for s in sc])
    pc = [jnp.where(v, jnp.exp(s - mc), 0.0) for s, v in zip(sc, valid)]
    lc = functools.reduce(jnp.add, [jnp.sum(p, axis=1, keepdims=True) for p in pc])
    inv_c = 1.0 / jnp.maximum(lc, 1e-30)
    pc = [p * inv_c for p in pc]
    halves = []
    for g in range(NSA_KV_HEADS):
        t = None
        for bb in range(bpp):
            r = jnp.dot(pc[bb].astype(BF16), cv[:, g * NSA_KVW:(g + 1) * NSA_KVW],
                        preferred_element_type=F32)
            r = jnp.where((lane >= bb * BLOCK) & (lane < (bb + 1) * BLOCK), r, 0.0)
            t = r if t is None else t + r
        halves.append(t + pltpu.roll(t, BLOCK, axis=1))
    oc = jnp.where(lane < NSA_HD, halves[0], halves[1])

    width = -(-(nb + 1) // 128) * 128
    imp = [jnp.concatenate([jnp.broadcast_to(
        jnp.sum(p[g * NSA_HPG:(g + 1) * NSA_HPG], axis=0, keepdims=True), (NSA_HPG, npages))
        for g in range(NSA_KV_HEADS)], axis=0) for p in pc]
    imp = jnp.concatenate(imp + [jnp.zeros((nh, width - nb), F32)], axis=1)
    li = lax.broadcasted_iota(jnp.int32, (nh, width), 1)
    assert npages & (npages - 1) == 0
    shift = npages.bit_length() - 1
    bi = jnp.where(li < nb, bpp * jnp.bitwise_and(li, npages - 1)
                   + lax.shift_right_logical(li, shift), li)
    bf = bi.astype(F32)
    causal = bi <= cur
    forced = jnp.logical_and(causal, jnp.logical_or(bi == 0, bi > cur - N_LOCAL))
    score = jnp.where(forced, -4.0, jnp.where(causal, imp, -1.0))
    score = jnp.where(li <= nb, score, -3.0)

    if nb + 1 <= N_SELECT:
        sel = jnp.where(li <= nb, 1.0, 0.0)
    else:
        cols = jnp.concatenate([score, bf, jnp.zeros((128 - 2 * nh, width), F32)], axis=0).T
        earlier = jnp.where(cols[:, nh:nh + 1] < bf[0:1], 1.0, 0.0)
        sels = []
        for g in range(NSA_KV_HEADS):
            scol = cols[:, g * NSA_HPG:g * NSA_HPG + 1]
            srow = score[g * NSA_HPG:g * NSA_HPG + 1]
            ahead = jnp.where(scol > srow, 1.0, jnp.where(scol == srow, earlier, 0.0))
            rank = jnp.sum(ahead, axis=0, keepdims=True)
            take = jnp.logical_and(rank < N_SELECT - 1 - N_LOCAL, srow > -2.0)
            sels.append(jnp.broadcast_to(jnp.where(take, 1.0, 0.0), (NSA_HPG, width)))
        sel = jnp.where(forced, 1.0, jnp.concatenate(sels, axis=0))
    selb = jnp.where(sel > 0.0, 0.0, NEG)
    bias = jnp.dot(selb[:, :nb].astype(BF16), exp_ref[...], preferred_element_type=F32)
    selb_new = selb[:, nb:nb + 1]

    s_parts = []
    for p in range(npages):
        kt = sbuf[p * _HALF_ROWS:p * _HALF_ROWS + NSA_KVW, :].astype(BF16)
        s_parts.append(jnp.dot(qb, kt, preferred_element_type=F32)
                       + bias[:, p * page:(p + 1) * page])
    s_new = jnp.sum(qf * _bf16_round(kvn[:, 2 * NSA_KVW:3 * NSA_KVW]), axis=1,
                    keepdims=True) + selb_new
    p_parts, p_new, inv = _softmax_rows(s_parts, s_new)
    acc = _bf16_round(p_new) * _bf16_round(kvn[:, 3 * NSA_KVW:])
    for p in range(npages):
        vt = sbuf[p * _HALF_ROWS + NSA_KVW:(p + 1) * _HALF_ROWS, :].astype(BF16)
        acc += lax.dot_general(p_parts[p].astype(BF16), vt, dims_t, preferred_element_type=F32)
    os_ = acc * inv

    win = win_ref[j]
    wlen = win.shape[1]
    sw = jnp.dot(qb, win[:NSA_KVW].astype(BF16), preferred_element_type=F32)
    wpos = pos - wlen + lax.broadcasted_iota(jnp.int32, (nh, wlen), 1)
    dist = pos - wpos
    sw = jnp.where(jnp.logical_and(jnp.logical_and(dist >= 0, dist <= WINDOW), wpos >= 0),
                   sw, NEG)
    sw_new = jnp.sum(qf * _bf16_round(wkvn[:, :NSA_KVW]), axis=1, keepdims=True)
    (pw,), pw_new, inv_w = _softmax_rows([sw], sw_new)
    ow = (lax.dot_general(pw.astype(BF16), win[NSA_KVW:].astype(BF16), dims_t,
                          preferred_element_type=F32)
          + _bf16_round(pw_new) * _bf16_round(wkvn[:, NSA_KVW:])) * inv_w

    gt = _sigmoid(gts_ref[j])
    o_ref[j] = gt[:, 0:1] * oc + gt[:, 1:2] * os_ + gt[:, 2:3] * ow

    wl = lax.broadcasted_iota(jnp.int32, win.shape, 1)
    wout_ref[j] = jnp.where(wl == wlen - 1, wcol_ref[j], pltpu.roll(win, wlen - 1, axis=1))


def _nsa_sample_t_kernel(pt_ref, cache_ref, qc_ref, qbd_ref, gts_ref, kvn_ref, wkvn_ref, wcol_ref,
                         win_ref, w2t_ref, pet_ref, exp_ref, o_ref, wout_ref, cbuf0, sbuf0, cbuf1,
                         sbuf1, sem, *, npages, page):
    s = pl.program_id(0)
    buf0 = (cbuf0, sbuf0)
    buf1 = (cbuf1, sbuf1)
    compute = functools.partial(_sample_compute, qc_ref=qc_ref, qbd_ref=qbd_ref, gts_ref=gts_ref,
                                kvn_ref=kvn_ref, wkvn_ref=wkvn_ref, wcol_ref=wcol_ref,
                                win_ref=win_ref, w2t_ref=w2t_ref, pet_ref=pet_ref, exp_ref=exp_ref,
                                o_ref=o_ref, wout_ref=wout_ref, npages=npages, page=page)
    copies = lambda seq, buf, slot: _page_copies(cache_ref, pt_ref, buf, sem.at[slot], seq, npages)

    @pl.when(s == 0)
    def _():
        for c in copies(0, buf0, 0):
            c.start()

    for c in copies(2 * s + 1, buf1, 1):
        c.start()
    for c in copies(2 * s, buf0, 0):
        c.wait()
    compute(buf0, 0)

    @pl.when(s + 1 < pl.num_programs(0))
    def _():
        for c in copies(2 * s + 2, buf0, 0):
            c.start()

    for c in copies(2 * s + 1, buf1, 1):
        c.wait()
    compute(buf1, 1)


def _nsa_sample_t(page_table, cache_t, qc, qbd, gts3, kvn3, wkvn3, wcol, win_t, w2t, pet, expm):
    nd, npages = page_table.shape
    page = cache_t.shape[2]
    wlen = win_t.shape[2]
    assert nd % 2 == 0 and page % BLOCK == 0
    two = lambda *tail: pl.BlockSpec((2,) + tail, lambda s, pt: (s,) + (0,) * len(tail))
    const = lambda a: pl.BlockSpec(a.shape, lambda s, pt: (0,) * a.ndim,
                                   pipeline_mode=pl.Buffered(1))
    grid_spec = pltpu.PrefetchScalarGridSpec(
        num_scalar_prefetch=1,
        grid=(nd // 2,),
        in_specs=[pl.BlockSpec(memory_space=pl.ANY),
                  two(page // BLOCK, NSA_HEADS, 2 * NSA_KVW), two(NSA_HEADS, NSA_KVW),
                  two(NSA_HEADS, NSA_KVW), two(1, 4 * NSA_KVW), two(1, 2 * NSA_KVW),
                  two(2 * NSA_KVW, 1), two(2 * NSA_KVW, wlen), const(w2t), const(pet), const(expm)],
        out_specs=[two(NSA_HEADS, NSA_KVW), two(2 * NSA_KVW, wlen)],
        scratch_shapes=[pltpu.VMEM((_HALF_ROWS, npages, page), F32),
                        pltpu.VMEM((npages * _HALF_ROWS, page), F32)] * 2
                       + [pltpu.SemaphoreType.DMA((2,))])
    return pl.pallas_call(
        functools.partial(_nsa_sample_t_kernel, npages=npages, page=page),
        grid_spec=grid_spec,
        out_shape=[jax.ShapeDtypeStruct((nd, NSA_HEADS, NSA_KVW), F32),
                   jax.ShapeDtypeStruct(win_t.shape, F32)],
        compiler_params=_cparams(("arbitrary",)),
        name="nsa_sample",
    )(page_table.reshape(-1), cache_t, qc, qbd, gts3, kvn3, wkvn3, wcol, win_t, w2t, pet, expm)


def _gla_sample_kernel(gq_ref, gk_ref, gv_ref, misc_ref, w2_ref, ba_ref, s_ref, o_ref, sout_ref,
                       *, ns):
    la = _log_decay(misc_ref[...], w2_ref, ba_ref)
    pad = jnp.zeros((128 - 3 * ns, GLA_KW), F32)
    zt = jnp.concatenate([gq_ref[...] * (GLA_DK ** -0.5), gk_ref[...], jnp.exp(la), pad],
                         axis=0).T
    v = gv_ref[...]
    for i in range(ns):
        outs = []
        for h in range(GLA_HEADS):
            hr = slice(h * GLA_DK, (h + 1) * GLA_DK)
            qc = zt[hr, i:i + 1]
            kc = zt[hr, ns + i:ns + i + 1]
            ac = zt[hr, 2 * ns + i:2 * ns + i + 1]
            s_new = ac * s_ref[i, hr, :] + kc * v[i:i + 1, h * GLA_DV:(h + 1) * GLA_DV]
            sout_ref[i, hr, :] = s_new
            outs.append(jnp.sum(qc * s_new, axis=0, keepdims=True))
        o_ref[i:i + 1, :] = jnp.concatenate(outs, axis=1)


def _gla_sample(gq, gk, gv, misc2d, w2, ba, state3, ns=32):
    nd = gq.shape[0]
    row = lambda w: pl.BlockSpec((ns, w), lambda i: (i, 0))
    st = pl.BlockSpec((ns, GLA_KW, GLA_DV), lambda i: (i, 0, 0))
    return pl.pallas_call(
        functools.partial(_gla_sample_kernel, ns=ns),
        grid=(nd // ns,),
        in_specs=[row(GLA_KW), row(GLA_KW), row(GLA_WIDTH), row(MISC_W),
                  _const_spec(w2.shape), _const_spec(ba.shape), st],
        out_specs=[row(GLA_WIDTH), st],
        out_shape=[jax.ShapeDtypeStruct((nd, GLA_WIDTH), F32),
                   jax.ShapeDtypeStruct(state3.shape, F32)],
        compiler_params=_cparams(("parallel",)),
        name="gla_sample",
    )(gq, gk, gv, misc2d, w2, ba, state3)


def kernel(x_prompt, x_sample, cache_kv, cache_win, state_gla, page_table, p_prompt, p_sample,
           g_attn, w_in, w_cmp, pe_cmp, g_nsa_out, w_gla_a2, b_gla_a, g_gla_out, w_out, g_ffn,
           w_gate_up, w_down, g_ple, w_ple_gate, w_ple_proj, g_final):
    assert g_attn.shape[0] == 1, "single-layer trunk"
    n, s, d = x_prompt.shape
    nd, ds, _ = x_sample.shape
    assert ds == 1 and s % Q_BLOCK == 0 and s >= WINDOW
    w1 = _reorder_w_in(w_in[0])
    fw = _ffn_weights(g_nsa_out[0], g_gla_out[0], w_out[0], g_ffn[0], w_gate_up[0], w_down[0],
                      g_ple[0], w_ple_gate[0], w_ple_proj[0], g_final)
    cw2, cpe = _cmp_weights(w_cmp[0], pe_cmp[0])
    gw2, gba = _gla_gate_weights(w_gla_a2[0], b_gla_a[0])

    xp = x_prompt.reshape(n * s, d)
    q, kv, wkv, gq, gk, gv, gr, misc, kvt, wkvt = _project(xp, g_attn[0], w1, n, min(512, s))
    cmat, ksl, vst, kw, vwt = _nsa_prep(kv, wkv, kvt, wkvt, cw2, cpe, n, s, min(4096, s))
    o_n = _nsa_prompt(q, misc, cmat, ksl, vst, kw, vwt, n, s)
    o_g, s_p = _gla_prompt(gq, gk, gv, misc, gw2, gba, n, s, min(512, s))
    y_p = _mixer_ffn(o_n, o_g, gr, xp, p_prompt[0].reshape(n * s, -1), fw, min(512, n * s))

    xs = x_sample.reshape(nd, d)
    qs, kvs, wkvs, gqs, gks, gvs, grs, miscs, kvts, _ = _project(xs, g_attn[0], w1, 1, nd)
    page = cache_kv.shape[2]
    npages = page_table.shape[1]
    cache_t = _to_channel_major(cache_kv[0])
    win_t = _to_channel_major(cache_win[0])
    q4 = qs.reshape(nd, NSA_KV_HEADS, NSA_HPG, NSA_HD)
    qbd = jnp.concatenate([_place(q4[:, g], g, NSA_KV_HEADS) for g in range(NSA_KV_HEADS)],
                          axis=1)
    bpp = page // BLOCK
    qc = jnp.stack([jnp.concatenate([_place(q4[:, g], g * bpp + bb, NSA_KV_HEADS * bpp)
                                     for g in range(NSA_KV_HEADS)], axis=1)
                    for bb in range(bpp)], axis=1)
    gts3 = jnp.pad(miscs[:, :N_GATES].reshape(nd, NSA_HEADS, 3), ((0, 0), (0, 0), (0, NSA_KVW - 3)))
    w2t, pet, expm = _sample_consts(w_cmp[0], pe_cmp[0], npages, page)
    o8, win_new = _nsa_sample_t(page_table, cache_t, qc, qbd, gts3, kvs.reshape(nd, 1, -1),
                                wkvs.reshape(nd, 1, -1), wkvs.reshape(nd, -1, 1), win_t,
                                w2t, pet, expm)
    o8 = o8.reshape(nd, NSA_KV_HEADS, NSA_HPG, NSA_KV_HEADS, NSA_HD)
    o_ns = jnp.stack([o8[:, 0, :, 0], o8[:, 1, :, 1]], axis=1).reshape(nd, NSA_WIDTH)
    o_gs, s_s = _gla_sample(gqs, gks, gvs, miscs, gw2, gba,
                            state_gla[0].reshape(nd, GLA_KW, GLA_DV))
    y_s = _mixer_ffn(o_ns, o_gs, grs, xs, p_sample[0].reshape(nd, -1), fw, nd)

    wkeep = min(WINDOW, s)
    return (y_p.reshape(n, s, d),
            y_s.reshape(nd, 1, d),
            _from_channel_major(kvt, 4)[None],
            _from_channel_major(wkvt[:, :, s - wkeep:], 2)[None],
            s_p.reshape(1, n, GLA_HEADS, GLA_DK, GLA_DV),
            _from_channel_major(kvts, 4).reshape(1, nd, 1, 4, NSA_KV_HEADS, NSA_HD),
            _from_channel_major(win_new, 2)[None],
            s_s.reshape(1, nd, GLA_HEADS, GLA_DK, GLA_DV))


def _to_channel_major(x):
    n, t = x.shape[:2]
    return jnp.transpose(x, (0, 2, 3, 4, 1)).reshape(n, -1, t)


def _from_channel_major(xt, c):
    n, _, t = xt.shape
    return jnp.transpose(xt.reshape(n, c, NSA_KV_HEADS, NSA_HD, t), (0, 4, 1, 2, 3))


def _place(x, slot, nslots):
    z = jnp.zeros_like(x)
    return jnp.concatenate([x if i == slot else z for i in range(nslots)], axis=-1)
```

```python
import functools

import numpy as np
import jax
import jax.numpy as jnp
from jax import lax
from jax.experimental import pallas as pl
from jax.experimental.pallas import tpu as pltpu

F32 = jnp.float32
BF16 = jnp.bfloat16

EPS = 1e-6
NEG = -1e30
LOG2E = 1.4426950408889634

NSA_HEADS = 8
NSA_KV_HEADS = 2
NSA_HPG = NSA_HEADS // NSA_KV_HEADS
NSA_HD = 64
NSA_WIDTH = NSA_HEADS * NSA_HD
NSA_KVW = NSA_KV_HEADS * NSA_HD
BLOCK = 64
N_SELECT = 16
N_LOCAL = 2
WINDOW = 512
Q_BLOCK = 128
FORCE_SCORE = float(NSA_HPG + 1)
GLA_HEADS = 4
GLA_DK = 64
GLA_DV = 128
GLA_WIDTH = GLA_HEADS * GLA_DV
GLA_KW = GLA_HEADS * GLA_DK
GLA_GATE_RANK = 16
GLA_GATE_TEMP = 16.0
GLA_CHUNK = 64
N_GATES = 3 * NSA_HEADS
MISC_W = 128

VMEM_LIMIT = 56 * 1024 * 1024


def _cparams(sem):
    return pltpu.CompilerParams(dimension_semantics=sem, vmem_limit_bytes=VMEM_LIMIT)


def _const_spec(shape):
    nd = len(shape)
    return pl.BlockSpec(shape, lambda *_: (0,) * nd, pipeline_mode=pl.Buffered(1))


def _sigmoid(x):
    return 1.0 / (1.0 + jnp.exp(-x))


def _rms(x, g):
    ms = jnp.mean(x * x, axis=-1, keepdims=True)
    return x * lax.rsqrt(ms + EPS) * g


def _split2(x):
    hi = x.astype(BF16)
    return hi, (x - hi.astype(F32)).astype(BF16)


_P_Q, _P_KV, _P_WKV, _P_GQ, _P_GK, _P_GV, _P_GR, _P_MISC, _P_END = (
    0, 512, 1024, 1280, 1536, 1792, 2304, 2816, 2944)


def _reorder_w_in(w_in):
    offs = np.cumsum([0, NSA_WIDTH, 4 * NSA_KVW, 2 * NSA_KVW, N_GATES, GLA_KW, GLA_KW,
                      GLA_WIDTH, GLA_GATE_RANK, GLA_WIDTH])
    q, kv, wkv, gts, gq, gk, gv, ga, gr = [w_in[:, offs[i]:offs[i + 1]] for i in range(9)]
    pad = jnp.zeros((w_in.shape[0], MISC_W - N_GATES - GLA_GATE_RANK), w_in.dtype)
    return jnp.concatenate([q, kv, wkv, gq, gk, gv, gr, gts, ga, pad], axis=1).astype(BF16)


def _proj_kernel(x_ref, g_ref, w_ref, wt_ref, q_ref, kv_ref, wkv_ref, gq_ref, gk_ref, gv_ref,
                 gr_ref, misc_ref, kvt_ref, wkvt_ref):
    xn = _rms(x_ref[...], g_ref[...]).astype(BF16)
    for ref, a, b in ((q_ref, _P_Q, _P_KV), (kv_ref, _P_KV, _P_WKV), (wkv_ref, _P_WKV, _P_GQ),
                      (gq_ref, _P_GQ, _P_GK), (gk_ref, _P_GK, _P_GV), (gv_ref, _P_GV, _P_GR),
                      (gr_ref, _P_GR, _P_MISC), (misc_ref, _P_MISC, _P_END)):
        ref[...] = jnp.dot(xn, w_ref[:, a:b], preferred_element_type=F32)
    dims_t = (((1,), (1,)), ((), ()))
    nkv = kvt_ref.shape[1]
    kvt_ref[0] = lax.dot_general(wt_ref[0:nkv, :], xn, dims_t, preferred_element_type=F32)
    wkvt_ref[0] = lax.dot_general(wt_ref[nkv:, :], xn, dims_t, preferred_element_type=F32)


def _project(x2d, g_attn, w1, n, tm):
    t, d = x2d.shape
    s = t // n
    nj = s // tm
    widths = (512, 512, 256, 256, 256, 512, 512, MISC_W)
    wt = w1[:, _P_KV:_P_GQ].T
    tspec = lambda rows: pl.BlockSpec((1, rows, tm), lambda i: (i // nj, 0, i % nj))
    return pl.pallas_call(
        _proj_kernel,
        grid=(t // tm,),
        in_specs=[pl.BlockSpec((tm, d), lambda i: (i, 0)),
                  _const_spec((1, d)),
                  _const_spec(w1.shape),
                  _const_spec(wt.shape)],
        out_specs=[pl.BlockSpec((tm, w), lambda i: (i, 0)) for w in widths]
                  + [tspec(4 * NSA_KVW), tspec(2 * NSA_KVW)],
        out_shape=[jax.ShapeDtypeStruct((t, w), F32) for w in widths]
                  + [jax.ShapeDtypeStruct((n, 4 * NSA_KVW, s), F32),
                     jax.ShapeDtypeStruct((n, 2 * NSA_KVW, s), F32)],
        compiler_params=_cparams(("parallel",)),
        name="proj",
    )(x2d, g_attn.reshape(1, d), w1, wt)


def _group_mean_sq(x, bd_ref, width):
    hi, lo = _split2(x * x)
    bd = bd_ref[...]
    s = jnp.dot(hi, bd, preferred_element_type=F32) + jnp.dot(lo, bd, preferred_element_type=F32)
    return s * (1.0 / width)


def _ffn_kernel(on_ref, og_ref, gr_ref, x_ref, p_ref, gn_ref, gg_ref, bd64_ref, bd128_ref,
                wo_ref, gffn_ref, wgu_ref, wd_ref, gple_ref, wpg_ref, wpp_ref, gfin_ref,
                y_ref, h_ref, acc_ref, *, n_ff):
    o_n = on_ref[...]
    o_g = og_ref[...]
    r = gr_ref[...]
    a_n = o_n * lax.rsqrt(_group_mean_sq(o_n, bd64_ref, NSA_HD) + EPS) * gn_ref[...]
    a_g = o_g * lax.rsqrt(_group_mean_sq(o_g, bd128_ref, GLA_DV) + EPS) * gg_ref[...]
    a_g = a_g * (r * _sigmoid(r))
    x1 = (x_ref[...]
          + jnp.dot(a_n.astype(BF16), wo_ref[0:NSA_WIDTH, :], preferred_element_type=F32)
          + jnp.dot(a_g.astype(BF16), wo_ref[NSA_WIDTH:, :], preferred_element_type=F32))
    h_ref[...] = _rms(x1, gffn_ref[...]).astype(BF16)
    acc_ref[...] = x1

    def body(c, carry):
        gu = jnp.dot(h_ref[...], wgu_ref[c], preferred_element_type=F32)
        half = gu.shape[1] // 2
        gt = gu[:, :half]
        act = (gt * _sigmoid(gt) * gu[:, half:]).astype(BF16)
        acc_ref[...] += jnp.dot(act, wd_ref[c], preferred_element_type=F32)
        return carry

    lax.fori_loop(0, n_ff, body, 0)
    x2 = acc_ref[...]
    gate = _sigmoid(jnp.dot(_rms(x2, gple_ref[...]).astype(BF16), wpg_ref[...],
                            preferred_element_type=F32))
    x3 = x2 + jnp.dot(p_ref[...].astype(BF16), wpp_ref[...], preferred_element_type=F32) * gate
    y_ref[...] = _rms(x3, gfin_ref[...])


def _block_diag_ones(n, width):
    idx = np.arange(n) // width
    return jnp.asarray((idx[:, None] == idx[None, :]).astype(np.float32), BF16)


def _ffn_weights(g_nsa_out, g_gla_out, w_out, g_ffn, w_gate_up, w_down, g_ple, w_ple_gate,
                 w_ple_proj, g_final, ff_chunk=256):
    d = w_out.shape[1]
    d_ff = w_down.shape[0]
    n_ff = d_ff // ff_chunk
    wg = w_gate_up[:, :d_ff].reshape(d, n_ff, ff_chunk)
    wu = w_gate_up[:, d_ff:].reshape(d, n_ff, ff_chunk)
    wgu = jnp.concatenate([wg, wu], axis=2).transpose(1, 0, 2).astype(BF16)
    wd = w_down.reshape(n_ff, ff_chunk, d).astype(BF16)
    return dict(gn=g_nsa_out.reshape(1, NSA_WIDTH), gg=g_gla_out.reshape(1, GLA_WIDTH),
                bd64=_block_diag_ones(NSA_WIDTH, NSA_HD), bd128=_block_diag_ones(GLA_WIDTH, GLA_DV),
                wo=w_out.astype(BF16), gffn=g_ffn.reshape(1, d), wgu=wgu, wd=wd,
                gple=g_ple.reshape(1, d), wpg=w_ple_gate.astype(BF16),
                wpp=w_ple_proj.astype(BF16), gfin=g_final.reshape(1, d))


def _mixer_ffn(o_n, o_g, gr, x2d, p2d, fw, tm):
    t, d = x2d.shape
    n_ff = fw["wgu"].shape[0]
    consts = [fw[k] for k in ("gn", "gg", "bd64", "bd128", "wo", "gffn", "wgu", "wd", "gple",
                              "wpg", "wpp", "gfin")]
    row = lambda w: pl.BlockSpec((tm, w), lambda i: (i, 0))
    return pl.pallas_call(
        functools.partial(_ffn_kernel, n_ff=n_ff),
        grid=(t // tm,),
        in_specs=[row(NSA_WIDTH), row(GLA_WIDTH), row(GLA_WIDTH), row(d), row(p2d.shape[1])]
                 + [_const_spec(c.shape) for c in consts],
        out_specs=row(d),
        out_shape=jax.ShapeDtypeStruct((t, d), F32),
        scratch_shapes=[pltpu.VMEM((tm, d), BF16), pltpu.VMEM((tm, d), F32)],
        compiler_params=_cparams(("parallel",)),
        name="mixer_ffn",
    )(o_n, o_g, gr, x2d, p2d, *consts)


def _cmp_weights(w_cmp, pe_cmp):
    z = jnp.zeros((BLOCK, NSA_HD, NSA_HD), w_cmp.dtype)

    def bd(w):
        return jnp.concatenate([jnp.concatenate([w, z], axis=2),
                                jnp.concatenate([z, w], axis=2)], axis=1)

    w2 = jnp.stack([bd(w_cmp[0]), bd(w_cmp[1])], axis=1).astype(BF16)
    pe = jnp.stack([jnp.concatenate([pe_cmp[0], pe_cmp[0]], axis=1),
                    jnp.concatenate([pe_cmp[1], pe_cmp[1]], axis=1)], axis=1)
    return w2, pe.reshape(BLOCK, 2, 1, NSA_KVW)


def _compress(xk_ref, xv_ref, w2_ref, pe_ref, nb):
    acc_k = jnp.zeros((nb, NSA_KVW), F32)
    acc_v = jnp.zeros((nb, NSA_KVW), F32)
    for l in range(BLOCK):
        xk = (xk_ref[pl.ds(l, nb, stride=BLOCK), :] + pe_ref[l, 0]).astype(BF16)
        xv = (xv_ref[pl.ds(l, nb, stride=BLOCK), :] + pe_ref[l, 1]).astype(BF16)
        acc_k += jnp.dot(xk, w2_ref[l, 0], preferred_element_type=F32)
        acc_v += jnp.dot(xv, w2_ref[l, 1], preferred_element_type=F32)
    return acc_k, acc_v


def _prep_kernel(ckin_ref, cvin_ref, ksin_ref, kwin_ref, vst_in_ref, vwt_in_ref, w2_ref, pe_ref,
                 c_ref, ks_ref, vst_ref, kw_ref, vwt_ref, *, tk):
    nb = tk // BLOCK
    ck, cv = _compress(ckin_ref, cvin_ref, w2_ref, pe_ref, nb)
    c_ref[:, 0:NSA_KVW] = ck
    c_ref[:, NSA_KVW:] = cv
    ks_ref[...] = ksin_ref[...].astype(BF16)
    kw_ref[...] = kwin_ref[...].astype(BF16)
    vst_ref[...] = vst_in_ref[...].astype(BF16)
    vwt_ref[...] = vwt_in_ref[...].astype(BF16)


def _nsa_prep(kv2d, wkv2d, kvt, wkvt, w2, pe, n, s, tk):
    t = n * s
    nj = s // tk
    kv_specs = [pl.BlockSpec((tk, NSA_KVW), (lambda b, j, c=c: (b * nj + j, c))) for c in range(3)]
    return pl.pallas_call(
        functools.partial(_prep_kernel, tk=tk),
        grid=(n, nj),
        in_specs=kv_specs + [pl.BlockSpec((tk, NSA_KVW), lambda b, j: (b * nj + j, 0)),
                             pl.BlockSpec((1, NSA_KVW, tk), lambda b, j: (b, 3, j)),
                             pl.BlockSpec((1, NSA_KVW, tk), lambda b, j: (b, 1, j)),
                             _const_spec(w2.shape), _const_spec(pe.shape)],
        out_specs=[pl.BlockSpec((tk // BLOCK, 2 * NSA_KVW), lambda b, j: (b * nj + j, 0)),
                   pl.BlockSpec((tk, NSA_KVW), lambda b, j: (b * nj + j, 0)),
                   pl.BlockSpec((1, NSA_KVW, tk), lambda b, j: (b, 0, j)),
                   pl.BlockSpec((tk, NSA_KVW), lambda b, j: (b * nj + j, 0)),
                   pl.BlockSpec((1, NSA_KVW, tk), lambda b, j: (b, 0, j))],
        out_shape=[jax.ShapeDtypeStruct((t // BLOCK, 2 * NSA_KVW), F32),
                   jax.ShapeDtypeStruct((t, NSA_KVW), BF16),
                   jax.ShapeDtypeStruct((n, NSA_KVW, s), BF16),
                   jax.ShapeDtypeStruct((t, NSA_KVW), BF16),
                   jax.ShapeDtypeStruct((n, NSA_KVW, s), BF16)],
        compiler_params=_cparams(("parallel", "parallel")),
        name="nsa_prep",
    )(kv2d, kv2d, kv2d, wkv2d, kvt, wkvt, w2, pe)


_LQ = NSA_KV_HEADS * NSA_HPG * Q_BLOCK
_LG = NSA_HPG * Q_BLOCK
_SEL_SHIFT = 2
_SEL_GROUP = Q_BLOCK << _SEL_SHIFT


def _rep_heads(x):
    a, b = x[:, :Q_BLOCK], x[:, Q_BLOCK:]
    return jnp.concatenate([a] * NSA_HPG + [b] * NSA_HPG, axis=1)


def _flash_update(s, vt, carry):
    m, l, acc0, acc1 = carry
    m_new = jnp.maximum(m, jnp.max(s, axis=0, keepdims=True))
    alpha = jnp.exp2(m - m_new)
    p = jnp.exp2(s - m_new)
    l = alpha * l + jnp.sum(p, axis=0, keepdims=True)
    pb = p.astype(BF16)
    acc0 = alpha[:, :_LG] * acc0 + jnp.dot(vt[:NSA_HD], pb[:, :_LG], preferred_element_type=F32)
    acc1 = alpha[:, _LG:] * acc1 + jnp.dot(vt[NSA_HD:], pb[:, _LG:], preferred_element_type=F32)
    return m_new, l, acc0, acc1


def _flash_init():
    return (jnp.full((1, _LQ), NEG, F32), jnp.zeros((1, _LQ), F32),
            jnp.zeros((NSA_HD, _LG), F32), jnp.zeros((NSA_HD, _LG), F32))


def _flash_out(carry):
    m, l, acc0, acc1 = carry
    inv = 1.0 / jnp.maximum(l, 1e-30)
    return acc0 * inv[:, :_LG], acc1 * inv[:, _LG:]


def _select_blocks(imps, cur, nbs):
    if nbs <= N_SELECT:
        return [jnp.ones(imp.shape, F32) for imp in imps]
    bi = lax.broadcasted_iota(jnp.int32, imps[0].shape, 0)
    bf = bi.astype(F32)
    causal = bi <= cur
    forced = jnp.logical_and(causal, jnp.logical_or(bi == 0, bi > cur - N_LOCAL))
    scores = tuple(jnp.where(forced, -2.0, jnp.where(causal, imp, -1.0)) for imp in imps)

    def take_one(score):
        mx = jnp.max(score, axis=0, keepdims=True)
        idx = jnp.min(jnp.where(score == mx, bf, float(nbs)), axis=0, keepdims=True)
        return jnp.where(bf == idx, -2.0, score)

    scores = lax.fori_loop(0, N_SELECT - 1 - N_LOCAL,
                           lambda _, sc: tuple(take_one(s) for s in sc), scores)
    return [jnp.where(s == -2.0, 1.0, 0.0) for s in scores]


def _nsa_prompt_kernel(q_ref, misc_ref, c_ref, ks_ref, vst_ref, ind_ref, indw_ref, *rest, nbc):
    kw_refs = rest[0:5]
    vwt_refs = rest[5:10]
    o_ref = rest[10]
    selb_ref, s0_ref, s1_ref, p0_ref, p1_ref = rest[11:16]
    i = pl.program_id(1)

    qT = (q_ref[...] * (NSA_HD ** -0.5 * LOG2E)).T
    z = jnp.zeros((NSA_HD, Q_BLOCK), F32)
    top = [qT[NSA_HD * h:NSA_HD * (h + 1)] for h in range(NSA_HPG)] + [z] * NSA_HPG
    bot = [z] * NSA_HPG + [qT[NSA_HD * h:NSA_HD * (h + 1)] for h in range(NSA_HPG, NSA_HEADS)]
    qt = jnp.concatenate([jnp.concatenate(top, axis=1), jnp.concatenate(bot, axis=1)],
                         axis=0).astype(BF16)

    cmat = c_ref[...]
    ck = cmat[:, :NSA_KVW].astype(BF16)
    cv = cmat[:, NSA_KVW:].astype(BF16)
    sc = jnp.dot(ck, qt, preferred_element_type=F32)
    b_io = lax.broadcasted_iota(jnp.int32, (nbc, _LQ), 0)
    qoff = jnp.bitwise_and(lax.broadcasted_iota(jnp.int32, (nbc, _LQ), 1), Q_BLOCK - 1)
    valid = b_io * BLOCK + (BLOCK - 1) <= i * Q_BLOCK + qoff
    sc = jnp.where(valid, sc, NEG)
    mc = jnp.max(sc, axis=0, keepdims=True)
    pc = jnp.where(valid, jnp.exp2(sc - mc), 0.0)
    pc = pc / jnp.maximum(jnp.sum(pc, axis=0, keepdims=True), 1e-30)
    oc_full = lax.dot_general(cv, pc.astype(BF16), (((0,), (0,)), ((), ())),
                              preferred_element_type=F32)
    oc = (oc_full[:NSA_HD, :_LG], oc_full[NSA_HD:, _LG:])

    qo = lax.broadcasted_iota(jnp.int32, (1, Q_BLOCK), 1)
    cur = 2 * i + (qo >= BLOCK).astype(jnp.int32)
    imps = []
    for g in range(NSA_KV_HEADS):
        imp = pc[:, g * _LG:g * _LG + Q_BLOCK]
        for hh in range(1, NSA_HPG):
            imp = imp + pc[:, g * _LG + hh * Q_BLOCK:g * _LG + (hh + 1) * Q_BLOCK]
        imps.append(imp)
    for g, sel in enumerate(_select_blocks(imps, cur, nbc)):
        selb_ref[:, g * Q_BLOCK:(g + 1) * Q_BLOCK] = jnp.where(sel > 0.0, 0.0, NEG)

    nbg = _SEL_GROUP // BLOCK
    ind = ind_ref[...]
    zpad = jnp.zeros((NSA_KVW - 16, _LQ), BF16)

    def scores(jg):
        off = pl.multiple_of(jg * _SEL_GROUP, _SEL_GROUP)
        ka = jnp.concatenate([ks_ref[pl.ds(off, _SEL_GROUP), :], ind], axis=1)
        brow = selb_ref[pl.ds(pl.multiple_of(jg * nbg, nbg), nbg), :]
        baug = jnp.concatenate([_rep_heads(brow), jnp.zeros((16 - nbg, _LQ), F32)],
                               axis=0).astype(BF16)
        return jnp.dot(ka, jnp.concatenate([qt, baug, zpad], axis=0), preferred_element_type=F32)

    def values(jg, p_ref):
        vt = vst_ref[0, :, pl.ds(pl.multiple_of(jg * _SEL_GROUP, _SEL_GROUP), _SEL_GROUP)]
        return (jnp.dot(vt[:NSA_HD], p_ref[:, :_LG], preferred_element_type=F32),
                jnp.dot(vt[NSA_HD:], p_ref[:, _LG:], preferred_element_type=F32))

    def stage(jg, s_cur, p_cur, s_nxt, p_prev, st, diagonal=False):
        m, l, b0, b1 = st
        if s_nxt is not None:
            s_nxt[...] = scores(jg + 1)
        pv0, pv1 = values(jnp.maximum(jg - 1, 0), p_prev)
        s = s_cur[...]
        if diagonal:
            rowg = lax.broadcasted_iota(jnp.int32, (_SEL_GROUP, _LQ), 0)
            qcg = jnp.bitwise_and(lax.broadcasted_iota(jnp.int32, (_SEL_GROUP, _LQ), 1),
                                  Q_BLOCK - 1)
            s = jnp.where(jg * _SEL_GROUP + rowg <= i * Q_BLOCK + qcg, s, NEG)
        m_new = jnp.maximum(m, jnp.max(s, axis=0, keepdims=True))
        alpha = jnp.exp2(m - m_new)
        p = jnp.exp2(s - m_new)
        p_cur[...] = p.astype(BF16)
        l = alpha * l + jnp.sum(p, axis=0, keepdims=True)
        return m_new, l, alpha[:, :_LG] * (b0 + pv0), alpha[:, _LG:] * (b1 + pv1)

    def finish(st, p_last, jg):
        m, l, b0, b1 = st
        pv0, pv1 = values(jg, p_last)
        inv = 1.0 / jnp.maximum(l, 1e-30)
        return (b0 + pv0) * inv[:, :_LG], (b1 + pv1) * inv[:, _LG:]

    nfull = lax.shift_right_logical(i, _SEL_SHIFT)
    s0_ref[...] = scores(0)
    p1_ref[...] = jnp.zeros(p1_ref.shape, BF16)

    def pair(t, st):
        st = stage(2 * t, s0_ref, p0_ref, s1_ref, p1_ref, st)
        return stage(2 * t + 1, s1_ref, p1_ref, s0_ref, p0_ref, st)

    st = lax.fori_loop(0, lax.shift_right_logical(nfull, 1), pair, _flash_init())

    def odd_tail(st):
        st = stage(nfull - 1, s0_ref, p0_ref, s1_ref, p1_ref, st)
        st = stage(nfull, s1_ref, p1_ref, None, p0_ref, st, diagonal=True)
        return finish(st, p1_ref, nfull)

    def even_tail(st):
        st = stage(nfull, s0_ref, p0_ref, None, p1_ref, st, diagonal=True)
        return finish(st, p0_ref, nfull)

    os_ = lax.cond(jnp.bitwise_and(nfull, 1) == 1, odd_tail, even_tail, st)

    nwt = len(kw_refs)
    row = lax.broadcasted_iota(jnp.int32, (Q_BLOCK, _LQ), 0)
    qcol = jnp.bitwise_and(lax.broadcasted_iota(jnp.int32, (Q_BLOCK, _LQ), 1), Q_BLOCK - 1)
    tile = lax.broadcasted_iota(jnp.int32, (16, _LQ), 0)
    wbias = jnp.where(tile < nwt - 1 - i, NEG, 0.0).astype(BF16)
    kwa = jnp.concatenate([jnp.concatenate([r[...] for r in kw_refs], axis=0), indw_ref[...]],
                          axis=1)
    sw = jnp.dot(kwa, jnp.concatenate([qt, wbias, zpad], axis=0), preferred_element_type=F32)
    sw = jnp.concatenate([jnp.where(row >= qcol, sw[:Q_BLOCK], NEG),
                          sw[Q_BLOCK:(nwt - 1) * Q_BLOCK],
                          jnp.where(row <= qcol, sw[(nwt - 1) * Q_BLOCK:], NEG)], axis=0)
    ow = _flash_out(_flash_update(sw, jnp.concatenate([r[0] for r in vwt_refs], axis=1),
                                  _flash_init()))

    gt = _sigmoid(misc_ref[...]).T
    outs = []
    for g in range(NSA_KV_HEADS):
        for hh in range(NSA_HPG):
            h = g * NSA_HPG + hh
            sl = slice(hh * Q_BLOCK, (hh + 1) * Q_BLOCK)
            outs.append(gt[3 * h:3 * h + 1] * oc[g][:, sl] + gt[3 * h + 1:3 * h + 2] * os_[g][:, sl]
                        + gt[3 * h + 2:3 * h + 3] * ow[g][:, sl])
    o_ref[...] = jnp.concatenate(outs, axis=0).T


def _nsa_prompt(q2d, misc2d, cmat, ks, vst, kw, vwt, n, s):
    assert s % _SEL_GROUP == 0
    nq = s // Q_BLOCK
    nbc = s // BLOCK
    blk = np.arange(_SEL_GROUP) // BLOCK
    ind = jnp.asarray((blk[:, None] == np.arange(NSA_KVW)[None, :]).astype(np.float32), BF16)
    wtile = np.arange(5 * Q_BLOCK) // Q_BLOCK
    indw = jnp.asarray((wtile[:, None] == np.arange(NSA_KVW)[None, :]).astype(np.float32), BF16)
    wk_specs = [pl.BlockSpec((Q_BLOCK, NSA_KVW),
                             (lambda b, i, t=t: (b * nq + jnp.maximum(i - 4 + t, 0), 0)))
                for t in range(5)]
    wv_specs = [pl.BlockSpec((1, NSA_KVW, Q_BLOCK),
                             (lambda b, i, t=t: (b, 0, jnp.maximum(i - 4 + t, 0))))
                for t in range(5)]
    return pl.pallas_call(
        functools.partial(_nsa_prompt_kernel, nbc=nbc),
        grid=(n, nq),
        in_specs=[pl.BlockSpec((Q_BLOCK, NSA_WIDTH), lambda b, i: (b * nq + i, 0)),
                  pl.BlockSpec((Q_BLOCK, MISC_W), lambda b, i: (b * nq + i, 0)),
                  pl.BlockSpec((nbc, 2 * NSA_KVW), lambda b, i: (b, 0)),
                  pl.BlockSpec((s, NSA_KVW), lambda b, i: (b, 0)),
                  pl.BlockSpec((1, NSA_KVW, s), lambda b, i: (b, 0, 0)),
                  _const_spec(ind.shape), _const_spec(indw.shape)] + wk_specs + wv_specs,
        out_specs=pl.BlockSpec((Q_BLOCK, NSA_WIDTH), lambda b, i: (b * nq + i, 0)),
        out_shape=jax.ShapeDtypeStruct((n * s, NSA_WIDTH), F32),
        scratch_shapes=[pltpu.VMEM((nbc, 2 * Q_BLOCK), F32),
                        pltpu.VMEM((_SEL_GROUP, _LQ), F32), pltpu.VMEM((_SEL_GROUP, _LQ), F32),
                        pltpu.VMEM((_SEL_GROUP, _LQ), BF16), pltpu.VMEM((_SEL_GROUP, _LQ), BF16)],
        compiler_params=_cparams(("parallel", "arbitrary")),
        name="nsa_prompt",
    )(q2d, misc2d, cmat, ks, vst, ind, indw, *([kw] * 5), *([vwt] * 5))


_N_LEVELS = int(np.log2(GLA_CHUNK))


def _gla_consts():
    c = GLA_CHUNK
    t = np.arange(c)
    mats = [t[None, :] <= t[:, None], t[None, :] > t[:, None]]
    masks = []
    w = c // 2
    while w >= 1:
        blk = t // (2 * w)
        mid = blk * 2 * w + w
        upper = t >= mid
        m = np.zeros((c, c), bool)
        for r in range(c):
            if upper[r]:
                m[r, mid[r]:r + 1] = True
            else:
                m[r, r + 1:mid[r]] = True
        mats.append(m)
        masks.append((blk[:, None] == blk[None, :]) & upper[:, None] & ~upper[None, :])
        w //= 2
    masks.append(np.eye(c, dtype=bool))
    mall = jnp.asarray(np.concatenate(mats, axis=0).astype(np.float32), BF16)
    lmask = jnp.asarray(np.tile(np.stack(masks).astype(np.float32), (1, GLA_HEADS, 1)))
    hd = np.arange(GLA_KW) // GLA_DK
    hmask = jnp.asarray((hd[:, None] == hd[None, :]).astype(np.float32))
    return mall, lmask, hmask


def _gla_gate_weights(w_a2, b_a):
    w2 = jnp.zeros((MISC_W, GLA_KW), F32).at[N_GATES:N_GATES + GLA_GATE_RANK].set(w_a2)
    return w2.astype(BF16), b_a.reshape(1, GLA_KW)


def _log_decay(misc, w2_ref, ba_ref):
    x = jnp.dot(misc.astype(BF16), w2_ref[...], preferred_element_type=F32) + ba_ref[...]
    return (jnp.minimum(x, 0.0) - jnp.log1p(jnp.exp(-jnp.abs(x)))) * (1.0 / GLA_GATE_TEMP)


def _gla_kernel(gq_ref, gk_ref, gv_ref, misc_ref, w2_ref, ba_ref, mall_ref, lmask_ref, hm_ref,
                o_ref, sout_ref, s_ref, *, nchunk):
    j = pl.program_id(1)

    @pl.when(j == 0)
    def _():
        s_ref[...] = jnp.zeros(s_ref.shape, F32)

    la_all = _log_decay(misc_ref[...], w2_ref, ba_ref)
    mall = mall_ref[...]
    hm = hm_ref[...]
    c = GLA_CHUNK
    dims_t = (((1,), (1,)), ((), ()))

    def stack(x):
        return (jnp.concatenate([x] * GLA_HEADS, axis=0) * hm).astype(BF16)

    for ci in range(nchunk):
        rows = slice(ci * c, (ci + 1) * c)
        q = gq_ref[rows, :] * (GLA_DK ** -0.5)
        k = gk_ref[rows, :]
        v = gv_ref[rows, :].astype(BF16)
        la = la_all[rows]
        hi, lo = _split2(la)
        ex = jnp.exp(jnp.dot(mall, hi, preferred_element_type=F32)
                     + jnp.dot(mall, lo, preferred_element_type=F32))
        att = lax.dot_general(stack(q), k.astype(BF16), dims_t,
                              preferred_element_type=F32) * lmask_ref[_N_LEVELS]
        for lev in range(_N_LEVELS):
            e = ex[(2 + lev) * c:(3 + lev) * c]
            att += lax.dot_general(stack(q * e), (k * e).astype(BF16), dims_t,
                                   preferred_element_type=F32) * lmask_ref[lev]
        attb = att.astype(BF16)
        s_old = s_ref[...]
        o_inter = jnp.dot(stack(q * ex[0:c]), s_old.astype(BF16), preferred_element_type=F32)
        kd = (k * ex[c:2 * c]).astype(BF16)
        upd = lax.dot_general(kd, v, (((0,), (0,)), ((), ())), preferred_element_type=F32)
        dec = jnp.exp(jnp.sum(la.T, axis=1, keepdims=True))
        outs, news = [], []
        for h in range(GLA_HEADS):
            hr = slice(h * GLA_DK, (h + 1) * GLA_DK)
            hv = slice(h * GLA_DV, (h + 1) * GLA_DV)
            outs.append(o_inter[hr] + jnp.dot(attb[hr], v[:, hv], preferred_element_type=F32))
            news.append(upd[hr, hv])
        o_ref[rows, :] = jnp.concatenate(outs, axis=1)
        s_ref[...] = dec * s_old + jnp.concatenate(news, axis=0)

    @pl.when(j == pl.num_programs(1) - 1)
    def _():
        sout_ref[0] = s_ref[...]


def _gla_prompt(gq, gk, gv, misc2d, w2, ba, n, s, tc):
    nj = s // tc
    mall, lmask, hmask = _gla_consts()
    row = lambda w: pl.BlockSpec((tc, w), lambda b, j: (b * nj + j, 0))
    return pl.pallas_call(
        functools.partial(_gla_kernel, nchunk=tc // GLA_CHUNK),
        grid=(n, nj),
        in_specs=[row(GLA_KW), row(GLA_KW), row(GLA_WIDTH), row(MISC_W),
                  _const_spec(w2.shape), _const_spec(ba.shape), _const_spec(mall.shape),
                  _const_spec(lmask.shape), _const_spec(hmask.shape)],
        out_specs=[row(GLA_WIDTH), pl.BlockSpec((1, GLA_KW, GLA_DV), lambda b, j: (b, 0, 0))],
        out_shape=[jax.ShapeDtypeStruct((n * s, GLA_WIDTH), F32),
                   jax.ShapeDtypeStruct((n, GLA_KW, GLA_DV), F32)],
        scratch_shapes=[pltpu.VMEM((GLA_KW, GLA_DV), F32)],
        compiler_params=_cparams(("parallel", "arbitrary")),
        name="gla_prompt",
    )(gq, gk, gv, misc2d, w2, ba, mall, lmask, hmask)


def _softmax_rows(s_parts, s_new):
    m = s_new
    for s in s_parts:
        m = jnp.maximum(m, jnp.max(s, axis=1, keepdims=True))
    p_parts = [jnp.exp(s - m) for s in s_parts]
    p_new = jnp.exp(s_new - m)
    l = p_new
    for p in p_parts:
        l = l + jnp.sum(p, axis=1, keepdims=True)
    return p_parts, p_new, 1.0 / jnp.maximum(l, 1e-30)


def _bf16_round(x):
    return x.astype(BF16).astype(F32)


_HALF_ROWS = 2 * NSA_KVW


def _page_copies(cache_ref, pt_ref, bufs, sem, seq, npages):
    cbuf, sbuf = bufs
    copies = []
    for p in range(npages):
        pg = pt_ref[seq * npages + p]
        copies.append(pltpu.make_async_copy(cache_ref.at[pg, pl.ds(0, _HALF_ROWS), :],
                                            cbuf.at[:, p, :], sem))
        copies.append(pltpu.make_async_copy(cache_ref.at[pg, pl.ds(_HALF_ROWS, _HALF_ROWS), :],
                                            sbuf.at[pl.ds(p * _HALF_ROWS, _HALF_ROWS), :], sem))
    return copies


def _sample_consts(w_cmp, pe_cmp, npages, page):
    bpp = page // BLOCK
    z = jnp.zeros((BLOCK, NSA_HD, NSA_HD), w_cmp.dtype)
    per_c = []
    for c in range(2):
        w = w_cmp[c].transpose(1, 0, 2)
        rows = [jnp.concatenate([w if b2 == b1 else z for b2 in range(bpp)], axis=2)
                for b1 in range(bpp)]
        per_c.append(jnp.concatenate(rows, axis=1))
    w2t = jnp.stack(per_c).astype(BF16)
    pet = jnp.stack([jnp.tile(pe_cmp[c].T, (1, bpp)) for c in range(2)])
    pet = pet.reshape(2, NSA_HD, 1, page)
    j = np.arange(bpp * npages)
    col = np.arange(npages * page)
    expm = (j[:, None] == ((col % page) // BLOCK) * npages + col // page)
    return w2t, pet, jnp.asarray(expm.astype(np.float32), BF16)


def _sample_compute(bufs, j, qc_ref, qbd_ref, gts_ref, kvn_ref, wkvn_ref, wcol_ref, win_ref,
                    w2t_ref, pet_ref, exp_ref, o_ref, wout_ref, *, npages, page):
    cbuf, sbuf = bufs
    bpp = page // BLOCK
    nb = npages * bpp
    pos = npages * page
    cur = pos // BLOCK
    nh = NSA_HEADS
    dims_t = (((1,), (1,)), ((), ()))
    lane = lax.broadcasted_iota(jnp.int32, (nh, NSA_KVW), 1)
    qf = _bf16_round(qbd_ref[j] * (NSA_HD ** -0.5))
    qb = qf.astype(BF16)
    kvn = kvn_ref[j]
    wkvn = wkvn_ref[j]

    cmp_ = []
    for c in range(2):
        acc = jnp.zeros((NSA_KV_HEADS * npages, page), F32)
        for d in range(NSA_HD):
            x = jnp.concatenate([cbuf[c * NSA_KVW + g * NSA_HD + d]
                                 for g in range(NSA_KV_HEADS)], axis=0)
            acc += jnp.dot((x + pet_ref[c, d]).astype(BF16), w2t_ref[c, d],
                           preferred_element_type=F32)
        cmp_.append(jnp.concatenate([acc[g * npages:(g + 1) * npages]
                                     for g in range(NSA_KV_HEADS)], axis=1).astype(BF16))
    ck, cv = cmp_

    pidx = lax.broadcasted_iota(jnp.int32, (nh, npages), 1)
    sc, valid = [], []
    for bb in range(bpp):
        s = lax.dot_general((qc_ref[j, bb] * (NSA_HD ** -0.5)).astype(BF16), ck, dims_t,
                            preferred_element_type=F32)
        v = (bpp * pidx + bb + 1) * BLOCK - 1 <= pos
        sc.append(jnp.where(v, s, NEG))
        valid.append(v)
    mc = functools.reduce(jnp.maximum, [jnp.max(s, axis=1, keepdims=True) for s in sc])
    pc = [jnp.where(v, jnp.exp(s - mc), 0.0) for s, v in zip(sc, valid)]
    lc = functools.reduce(jnp.add, [jnp.sum(p, axis=1, keepdims=True) for p in pc])
    inv_c = 1.0 / jnp.maximum(lc, 1e-30)
    pc = [p * inv_c for p in pc]
    halves = []
    for g in range(NSA_KV_HEADS):
        t = None
        for bb in range(bpp):
            r = jnp.dot(pc[bb].astype(BF16), cv[:, g * NSA_KVW:(g + 1) * NSA_KVW],
                        preferred_element_type=F32)
            r = jnp.where((lane >= bb * BLOCK) & (lane < (bb + 1) * BLOCK), r, 0.0)
            t = r if t is None else t + r
        halves.append(t + pltpu.roll(t, BLOCK, axis=1))
    oc = jnp.where(lane < NSA_HD, halves[0], halves[1])

    width = -(-(nb + 1) // 128) * 128
    imp = [jnp.concatenate([jnp.broadcast_to(
        jnp.sum(p[g * NSA_HPG:(g + 1) * NSA_HPG], axis=0, keepdims=True), (NSA_HPG, npages))
        for g in range(NSA_KV_HEADS)], axis=0) for p in pc]
    imp = jnp.concatenate(imp + [jnp.zeros((nh, width - nb), F32)], axis=1)
    li = lax.broadcasted_iota(jnp.int32, (nh, width), 1)
    assert npages & (npages - 1) == 0
    shift = npages.bit_length() - 1
    bi = jnp.where(li < nb, bpp * jnp.bitwise_and(li, npages - 1)
                   + lax.shift_right_logical(li, shift), li)
    bf = bi.astype(F32)
    causal = bi <= cur
    forced = jnp.logical_and(causal, jnp.logical_or(bi == 0, bi > cur - N_LOCAL))
    score = jnp.where(forced, -4.0, jnp.where(causal, imp, -1.0))
    score = jnp.where(li <= nb, score, -3.0)

    if nb + 1 <= N_SELECT:
        sel = jnp.where(li <= nb, 1.0, 0.0)
    else:
        cols = jnp.concatenate([score, bf, jnp.zeros((128 - 2 * nh, width), F32)], axis=0).T
        earlier = jnp.where(cols[:, nh:nh + 1] < bf[0:1], 1.0, 0.0)
        sels = []
        for g in range(NSA_KV_HEADS):
            scol = cols[:, g * NSA_HPG:g * NSA_HPG + 1]
            srow = score[g * NSA_HPG:g * NSA_HPG + 1]
            ahead = jnp.where(scol > srow, 1.0, jnp.where(scol == srow, earlier, 0.0))
            rank = jnp.sum(ahead, axis=0, keepdims=True)
            take = jnp.logical_and(rank < N_SELECT - 1 - N_LOCAL, srow > -2.0)
            sels.append(jnp.broadcast_to(jnp.where(take, 1.0, 0.0), (NSA_HPG, width)))
        sel = jnp.where(forced, 1.0, jnp.concatenate(sels, axis=0))
    selb = jnp.where(sel > 0.0, 0.0, NEG)
    bias = jnp.dot(selb[:, :nb].astype(BF16), exp_ref[...], preferred_element_type=F32)
    selb_new = selb[:, nb:nb + 1]

    s_parts = []
    for p in range(npages):
        kt = sbuf[p * _HALF_ROWS:p * _HALF_ROWS + NSA_KVW, :].astype(BF16)
        s_parts.append(jnp.dot(qb, kt, preferred_element_type=F32)
                       + bias[:, p * page:(p + 1) * page])
    s_new = jnp.sum(qf * _bf16_round(kvn[:, 2 * NSA_KVW:3 * NSA_KVW]), axis=1,
                    keepdims=True) + selb_new
    p_parts, p_new, inv = _softmax_rows(s_parts, s_new)
    acc = _bf16_round(p_new) * _bf16_round(kvn[:, 3 * NSA_KVW:])
    for p in range(npages):
        vt = sbuf[p * _HALF_ROWS + NSA_KVW:(p + 1) * _HALF_ROWS, :].astype(BF16)
        acc += lax.dot_general(p_parts[p].astype(BF16), vt, dims_t, preferred_element_type=F32)
    os_ = acc * inv

    win = win_ref[j]
    wlen = win.shape[1]
    sw = jnp.dot(qb, win[:NSA_KVW].astype(BF16), preferred_element_type=F32)
    wpos = pos - wlen + lax.broadcasted_iota(jnp.int32, (nh, wlen), 1)
    dist = pos - wpos
    sw = jnp.where(jnp.logical_and(jnp.logical_and(dist >= 0, dist <= WINDOW), wpos >= 0),
                   sw, NEG)
    sw_new = jnp.sum(qf * _bf16_round(wkvn[:, :NSA_KVW]), axis=1, keepdims=True)
    (pw,), pw_new, inv_w = _softmax_rows([sw], sw_new)
    ow = (lax.dot_general(pw.astype(BF16), win[NSA_KVW:].astype(BF16), dims_t,
                          preferred_element_type=F32)
          + _bf16_round(pw_new) * _bf16_round(wkvn[:, NSA_KVW:])) * inv_w

    gt = _sigmoid(gts_ref[j])
    o_ref[j] = gt[:, 0:1] * oc + gt[:, 1:2] * os_ + gt[:, 2:3] * ow

    wl = lax.broadcasted_iota(jnp.int32, win.shape, 1)
    wout_ref[j] = jnp.where(wl == wlen - 1, wcol_ref[j], pltpu.roll(win, wlen - 1, axis=1))


def _nsa_sample_t_kernel(pt_ref, cache_ref, qc_ref, qbd_ref, gts_ref, kvn_ref, wkvn_ref, wcol_ref,
                         win_ref, w2t_ref, pet_ref, exp_ref, o_ref, wout_ref, cbuf0, sbuf0, cbuf1,
                         sbuf1, sem, *, npages, page):
    s = pl.program_id(0)
    buf0 = (cbuf0, sbuf0)
    buf1 = (cbuf1, sbuf1)
    compute = functools.partial(_sample_compute, qc_ref=qc_ref, qbd_ref=qbd_ref, gts_ref=gts_ref,
                                kvn_ref=kvn_ref, wkvn_ref=wkvn_ref, wcol_ref=wcol_ref,
                                win_ref=win_ref, w2t_ref=w2t_ref, pet_ref=pet_ref, exp_ref=exp_ref,
                                o_ref=o_ref, wout_ref=wout_ref, npages=npages, page=page)
    copies = lambda seq, buf, slot: _page_copies(cache_ref, pt_ref, buf, sem.at[slot], seq, npages)

    def start_all(cs):
        for k, c in enumerate(cs):
            c.start(priority=k % 2)

    @pl.when(s == 0)
    def _():
        start_all(copies(0, buf0, 0))

    start_all(copies(2 * s + 1, buf1, 1))
    for c in copies(2 * s, buf0, 0):
        c.wait()
    compute(buf0, 0)

    @pl.when(s + 1 < pl.num_programs(0))
    def _():
        start_all(copies(2 * s + 2, buf0, 0))

    for c in copies(2 * s + 1, buf1, 1):
        c.wait()
    compute(buf1, 1)


def _nsa_sample_t(page_table, cache_t, qc, qbd, gts3, kvn3, wkvn3, wcol, win_t, w2t, pet, expm):
    nd, npages = page_table.shape
    page = cache_t.shape[2]
    wlen = win_t.shape[2]
    assert nd % 2 == 0 and page % BLOCK == 0
    two = lambda *tail: pl.BlockSpec((2,) + tail, lambda s, pt: (s,) + (0,) * len(tail))
    const = lambda a: pl.BlockSpec(a.shape, lambda s, pt: (0,) * a.ndim,
                                   pipeline_mode=pl.Buffered(1))
    grid_spec = pltpu.PrefetchScalarGridSpec(
        num_scalar_prefetch=1,
        grid=(nd // 2,),
        in_specs=[pl.BlockSpec(memory_space=pl.ANY),
                  two(page // BLOCK, NSA_HEADS, 2 * NSA_KVW), two(NSA_HEADS, NSA_KVW),
                  two(NSA_HEADS, NSA_KVW), two(1, 4 * NSA_KVW), two(1, 2 * NSA_KVW),
                  two(2 * NSA_KVW, 1), two(2 * NSA_KVW, wlen), const(w2t), const(pet), const(expm)],
        out_specs=[two(NSA_HEADS, NSA_KVW), two(2 * NSA_KVW, wlen)],
        scratch_shapes=[pltpu.VMEM((_HALF_ROWS, npages, page), F32),
                        pltpu.VMEM((npages * _HALF_ROWS, page), F32)] * 2
                       + [pltpu.SemaphoreType.DMA((2,))])
    return pl.pallas_call(
        functools.partial(_nsa_sample_t_kernel, npages=npages, page=page),
        grid_spec=grid_spec,
        out_shape=[jax.ShapeDtypeStruct((nd, NSA_HEADS, NSA_KVW), F32),
                   jax.ShapeDtypeStruct(win_t.shape, F32)],
        compiler_params=_cparams(("arbitrary",)),
        name="nsa_sample",
    )(page_table.reshape(-1), cache_t, qc, qbd, gts3, kvn3, wkvn3, wcol, win_t, w2t, pet, expm)


def _gla_sample_kernel(gq_ref, gk_ref, gv_ref, misc_ref, w2_ref, ba_ref, s_ref, o_ref, sout_ref,
                       *, ns):
    la = _log_decay(misc_ref[...], w2_ref, ba_ref)
    pad = jnp.zeros((128 - 3 * ns, GLA_KW), F32)
    zt = jnp.concatenate([gq_ref[...] * (GLA_DK ** -0.5), gk_ref[...], jnp.exp(la), pad],
                         axis=0).T
    v = gv_ref[...]
    for i in range(ns):
        outs = []
        for h in range(GLA_HEADS):
            hr = slice(h * GLA_DK, (h + 1) * GLA_DK)
            qc = zt[hr, i:i + 1]
            kc = zt[hr, ns + i:ns + i + 1]
            ac = zt[hr, 2 * ns + i:2 * ns + i + 1]
            s_new = ac * s_ref[i, hr, :] + kc * v[i:i + 1, h * GLA_DV:(h + 1) * GLA_DV]
            sout_ref[i, hr, :] = s_new
            outs.append(jnp.sum(qc * s_new, axis=0, keepdims=True))
        o_ref[i:i + 1, :] = jnp.concatenate(outs, axis=1)


def _gla_sample(gq, gk, gv, misc2d, w2, ba, state3, ns=32):
    nd = gq.shape[0]
    row = lambda w: pl.BlockSpec((ns, w), lambda i: (i, 0))
    st = pl.BlockSpec((ns, GLA_KW, GLA_DV), lambda i: (i, 0, 0))
    return pl.pallas_call(
        functools.partial(_gla_sample_kernel, ns=ns),
        grid=(nd // ns,),
        in_specs=[row(GLA_KW), row(GLA_KW), row(GLA_WIDTH), row(MISC_W),
                  _const_spec(w2.shape), _const_spec(ba.shape), st],
        out_specs=[row(GLA_WIDTH), st],
        out_shape=[jax.ShapeDtypeStruct((nd, GLA_WIDTH), F32),
                   jax.ShapeDtypeStruct(state3.shape, F32)],
        compiler_params=_cparams(("parallel",)),
        name="gla_sample",
    )(gq, gk, gv, misc2d, w2, ba, state3)


def kernel(x_prompt, x_sample, cache_kv, cache_win, state_gla, page_table, p_prompt, p_sample,
           g_attn, w_in, w_cmp, pe_cmp, g_nsa_out, w_gla_a2, b_gla_a, g_gla_out, w_out, g_ffn,
           w_gate_up, w_down, g_ple, w_ple_gate, w_ple_proj, g_final):
    assert g_attn.shape[0] == 1, "single-layer trunk"
    n, s, d = x_prompt.shape
    nd, ds, _ = x_sample.shape
    assert ds == 1 and s % Q_BLOCK == 0 and s >= WINDOW
    w1 = _reorder_w_in(w_in[0])
    fw = _ffn_weights(g_nsa_out[0], g_gla_out[0], w_out[0], g_ffn[0], w_gate_up[0], w_down[0],
                      g_ple[0], w_ple_gate[0], w_ple_proj[0], g_final)
    cw2, cpe = _cmp_weights(w_cmp[0], pe_cmp[0])
    gw2, gba = _gla_gate_weights(w_gla_a2[0], b_gla_a[0])

    xp = x_prompt.reshape(n * s, d)
    q, kv, wkv, gq, gk, gv, gr, misc, kvt, wkvt = _project(xp, g_attn[0], w1, n, min(512, s))
    cmat, ksl, vst, kw, vwt = _nsa_prep(kv, wkv, kvt, wkvt, cw2, cpe, n, s, min(4096, s))
    o_n = _nsa_prompt(q, misc, cmat, ksl, vst, kw, vwt, n, s)
    o_g, s_p = _gla_prompt(gq, gk, gv, misc, gw2, gba, n, s, min(512, s))
    y_p = _mixer_ffn(o_n, o_g, gr, xp, p_prompt[0].reshape(n * s, -1), fw, min(512, n * s))

    xs = x_sample.reshape(nd, d)
    qs, kvs, wkvs, gqs, gks, gvs, grs, miscs, kvts, _ = _project(xs, g_attn[0], w1, 1, nd)
    page = cache_kv.shape[2]
    npages = page_table.shape[1]
    cache_t = _to_channel_major(cache_kv[0])
    win_t = _to_channel_major(cache_win[0])
    q4 = qs.reshape(nd, NSA_KV_HEADS, NSA_HPG, NSA_HD)
    qbd = jnp.concatenate([_place(q4[:, g], g, NSA_KV_HEADS) for g in range(NSA_KV_HEADS)],
                          axis=1)
    bpp = page // BLOCK
    qc = jnp.stack([jnp.concatenate([_place(q4[:, g], g * bpp + bb, NSA_KV_HEADS * bpp)
                                     for g in range(NSA_KV_HEADS)], axis=1)
                    for bb in range(bpp)], axis=1)
    gts3 = jnp.pad(miscs[:, :N_GATES].reshape(nd, NSA_HEADS, 3), ((0, 0), (0, 0), (0, NSA_KVW - 3)))
    w2t, pet, expm = _sample_consts(w_cmp[0], pe_cmp[0], npages, page)
    o8, win_new = _nsa_sample_t(page_table, cache_t, qc, qbd, gts3, kvs.reshape(nd, 1, -1),
                                wkvs.reshape(nd, 1, -1), wkvs.reshape(nd, -1, 1), win_t,
                                w2t, pet, expm)
    o8 = o8.reshape(nd, NSA_KV_HEADS, NSA_HPG, NSA_KV_HEADS, NSA_HD)
    o_ns = jnp.stack([o8[:, 0, :, 0], o8[:, 1, :, 1]], axis=1).reshape(nd, NSA_WIDTH)
    o_gs, s_s = _gla_sample(gqs, gks, gvs, miscs, gw2, gba,
                            state_gla[0].reshape(nd, GLA_KW, GLA_DV))
    y_s = _mixer_ffn(o_ns, o_gs, grs, xs, p_sample[0].reshape(nd, -1), fw, nd)

    wkeep = min(WINDOW, s)
    return (y_p.reshape(n, s, d),
            y_s.reshape(nd, 1, d),
            _from_channel_major(kvt, 4)[None],
            _from_channel_major(wkvt[:, :, s - wkeep:], 2)[None],
            s_p.reshape(1, n, GLA_HEADS, GLA_DK, GLA_DV),
            _from_channel_major(kvts, 4).reshape(1, nd, 1, 4, NSA_KV_HEADS, NSA_HD),
            _from_channel_major(win_new, 2)[None],
            s_s.reshape(1, nd, GLA_HEADS, GLA_DK, GLA_DV))


def _to_channel_major(x):
    n, t = x.shape[:2]
    return jnp.transpose(x, (0, 2, 3, 4, 1)).reshape(n, -1, t)


def _from_channel_major(xt, c):
    n, _, t = xt.shape
    return jnp.transpose(xt.reshape(n, c, NSA_KV_HEADS, NSA_HD, t), (0, 4, 1, 2, 3))


def _place(x, slot, nslots):
    z = jnp.zeros_like(x)
    return jnp.concatenate([x if i == slot else z for i in range(nslots)], axis=-1)
```
